```python
import numpy as np
import jax
import jax.numpy as jnp
from jax import lax

D_MODEL = 2048
BATCH = 8
SEQ = 2048
DEPTH = 1

CHUNK = 64
N_MEM = 256
NORM_EPS = 1e-6
NEG_INF = -1e30

A_HEADS = 8
A_HEAD_DIM = 128
A_WIDTH = A_HEADS * A_HEAD_DIM
A_LEFT_CHUNKS = 8
A_BAND = (A_LEFT_CHUNKS + 1) * CHUNK
REL_CLIP = 128

B_HEADS = 16
B_HEAD_DIM = 64
B_WIDTH = B_HEADS * B_HEAD_DIM
DECAY_LORA = 64
ICLR_LORA = 64
GN_EPS = 64e-5

C_HEADS = 4
C_HEAD_DIM = 256
C_WIDTH = C_HEADS * C_HEAD_DIM

IN_SPLITS = (A_WIDTH,) * 4 + (B_WIDTH,) * 4 + (DECAY_LORA, ICLR_LORA) + (C_WIDTH,) * 2 + (D_MODEL,) * 3
IN_COLS = sum(IN_SPLITS)

kernel_name = "hybrid_chunkattn_rwkv7_memxattn"


def rms_norm(x, g):
    xf = x.astype(jnp.float32)
    y = xf * lax.rsqrt(jnp.mean(xf * xf, axis=-1, keepdims=True) + NORM_EPS)
    return (y * g.astype(jnp.float32)).astype(x.dtype)


def _rel_index():
    r = np.arange(CHUNK)[:, None]
    m = np.arange(A_BAND)[None, :]
    dist = r + A_LEFT_CHUNKS * CHUNK - m
    return np.clip(dist, -REL_CLIP, REL_CLIP) + REL_CLIP


def chunked_band_attention(q, k, v, rel_bias):
    bsz, s, h, dh = q.shape
    n_chunks = s // CHUNK
    pad = A_LEFT_CHUNKS * CHUNK
    kp = jnp.pad(k, ((0, 0), (pad, 0), (0, 0), (0, 0)))
    vp = jnp.pad(v, ((0, 0), (pad, 0), (0, 0), (0, 0)))
    bias = rel_bias[:, _rel_index()].astype(jnp.float32)
    band_pos = jnp.arange(A_BAND) - pad
    scale = dh ** -0.5

    def one_chunk(c):
        start = c * CHUNK
        qc = lax.dynamic_slice_in_dim(q, start, CHUNK, axis=1)
        kc = lax.dynamic_slice_in_dim(kp, start, A_BAND, axis=1)
        vc = lax.dynamic_slice_in_dim(vp, start, A_BAND, axis=1)
        sc = jnp.einsum('bqhd,bkhd->bhqk', qc, kc, preferred_element_type=jnp.float32) * scale + bias
        valid = (start + band_pos) >= 0
        sc = jnp.where(valid[None, None, None, :], sc, NEG_INF)
        p = jax.nn.softmax(sc, axis=-1)
        return jnp.einsum('bhqk,bkhd->bqhd', p.astype(vc.dtype), vc)

    out = lax.map(one_chunk, jnp.arange(n_chunks))
    return jnp.transpose(out, (1, 0, 2, 3, 4)).reshape(bsz, s, h * dh)


def token_shift_lerp(p, mu):
    prev = jnp.pad(p, ((0, 0), (1, 0), (0, 0)))[:, :-1]
    return p + mu * (prev - p)


def rwkv7_scan(r, decay, k, v, a_vec, b_vec):
    bsz, _, h, n = r.shape

    def step(state, inp):
        r_t, w_t, k_t, v_t, a_t, b_t = inp
        sa = jnp.einsum('bhij,bhj->bhi', state, a_t)
        state = state * w_t[:, :, None, :] + sa[..., None] * b_t[:, :, None, :] + v_t[..., None] * k_t[:, :, None, :]
        return state, jnp.einsum('bhij,bhj->bhi', state, r_t)

    xs = tuple(jnp.moveaxis(t, 1, 0) for t in (r, decay, k, v, a_vec, b_vec))
    s0 = jnp.zeros((bsz, h, n, n), jnp.float32)
    _, out = lax.scan(step, s0, xs)
    return jnp.moveaxis(out, 0, 1)


def rwkv7_time_mix(p_r, p_k, p_v, p_wd, p_ad, mu_rkv, mu_w, mu_a, w0, w2, a0, a2, k_k, k_a, r_k, ln_w, ln_b):
    f = lambda t: t.astype(jnp.float32)
    p_r, p_k, p_v, p_wd, p_ad = f(p_r), f(p_k), f(p_v), f(p_wd), f(p_ad)
    bsz, s, _ = p_r.shape
    r = token_shift_lerp(p_r, f(mu_rkv[0]))
    k = token_shift_lerp(p_k, f(mu_rkv[1]))
    v = token_shift_lerp(p_v, f(mu_rkv[2]))
    wd = token_shift_lerp(p_wd, f(mu_w))
    ad = token_shift_lerp(p_ad, f(mu_a))
    w = -jax.nn.softplus(-(f(w0) + jnp.tanh(wd) @ f(w2))) - 0.5
    decay = jnp.exp(-jnp.exp(w))
    a = jax.nn.sigmoid(f(a0) + ad @ f(a2))
    heads = lambda t: t.reshape(bsz, s, B_HEADS, B_HEAD_DIM)
    kk = heads(k * f(k_k))
    kk = kk * lax.rsqrt(jnp.maximum(jnp.sum(kk * kk, axis=-1, keepdims=True), 1e-24))
    k = k * (1.0 + (a - 1.0) * f(k_a))
    rh, kh, vh, ah, dh = heads(r), heads(k), heads(v), heads(a), heads(decay)
    o = rwkv7_scan(rh, dh, kh, vh, -kk, kk * ah)
    mean = jnp.mean(o, axis=-1, keepdims=True)
    var = jnp.mean(jnp.square(o - mean), axis=-1, keepdims=True)
    o = ((o - mean) * lax.rsqrt(var + GN_EPS)).reshape(bsz, s, B_WIDTH) * f(ln_w) + f(ln_b)
    bonus = jnp.sum(rh * kh * f(r_k), axis=-1, keepdims=True) * vh
    return o + bonus.reshape(bsz, s, B_WIDTH)


def memory_cross_attention(q, mk, mv):
    bsz, s, h, dh = q.shape
    sc = jnp.einsum('bshd,bmhd->bhsm', q, mk, preferred_element_type=jnp.float32) * (dh ** -0.5)
    p = jax.nn.softmax(sc, axis=-1)
    out = jnp.einsum('bhsm,bmhd->bshd', p.astype(mv.dtype), mv)
    return out.reshape(bsz, s, h * dh)


def hybrid_layer(x, mem, norm_g, w_in, a_q_g, a_k_g, a_rel_bias, w_up_a,
                 b_mu_rkv, b_mu_w, b_mu_a, b_w0, b_w2, b_a0, b_a2, b_k_k, b_k_a, b_r_k,
                 b_ln_w, b_ln_b, w_up_b, mem_norm_g, w_mem_kv, c_q_g, c_k_g, w_up_c, w_o):
    bsz, s, _ = x.shape
    h = rms_norm(x, norm_g)
    proj = h @ w_in
    offsets = np.cumsum(np.array(IN_SPLITS))[:-1].tolist()
    (aq, ak, av, az, br, bk, bv, bz, bwd, bad, cq, cz, ga, gb, gc) = jnp.split(proj, offsets, axis=-1)

    aq = rms_norm(aq.reshape(bsz, s, A_HEADS, A_HEAD_DIM), a_q_g)
    ak = rms_norm(ak.reshape(bsz, s, A_HEADS, A_HEAD_DIM), a_k_g)
    av = av.reshape(bsz, s, A_HEADS, A_HEAD_DIM)
    ya = chunked_band_attention(aq, ak, av, a_rel_bias) * jax.nn.silu(az)

    yb = rwkv7_time_mix(br, bk, bv, bwd, bad, b_mu_rkv, b_mu_w, b_mu_a, b_w0, b_w2, b_a0, b_a2,
                        b_k_k, b_k_a, b_r_k, b_ln_w, b_ln_b).astype(x.dtype) * jax.nn.silu(bz)

    m = rms_norm(mem, mem_norm_g)
    mk, mv = jnp.split(m @ w_mem_kv, 2, axis=-1)
    mk = rms_norm(mk.reshape(bsz, N_MEM, C_HEADS, C_HEAD_DIM), c_k_g)
    mv = mv.reshape(bsz, N_MEM, C_HEADS, C_HEAD_DIM)
    cq = rms_norm(cq.reshape(bsz, s, C_HEADS, C_HEAD_DIM), c_q_g)
    yc = memory_cross_attention(cq, mk, mv) * jax.nn.silu(cz)

    merged = (jax.nn.sigmoid(ga) * (ya @ w_up_a)
              + jax.nn.sigmoid(gb) * (yb @ w_up_b)
              + jax.nn.sigmoid(gc) * (yc @ w_up_c))
    return x + merged @ w_o


def setup_inputs(seed: int = 0) -> dict:
    key = jax.random.key(seed)
    ks = jax.random.split(key, 32)
    nrm = lambda k, shape, sc: jax.random.normal(k, shape, jnp.float32) * sc
    L = DEPTH
    return {
        "x": nrm(ks[0], (BATCH, SEQ, D_MODEL), 1.0),
        "mem": nrm(ks[1], (BATCH, N_MEM, D_MODEL), 1.0),
        "norm_g": 1.0 + nrm(ks[2], (L, D_MODEL), 0.1),
        "w_in": nrm(ks[3], (L, D_MODEL, IN_COLS), D_MODEL ** -0.5),
        "a_q_g": 1.0 + nrm(ks[4], (L, A_HEAD_DIM), 0.1),
        "a_k_g": 1.0 + nrm(ks[5], (L, A_HEAD_DIM), 0.1),
        "a_rel_bias": nrm(ks[6], (L, A_HEADS, 2 * REL_CLIP + 1), 0.5),
        "w_up_a": nrm(ks[7], (L, A_WIDTH, D_MODEL), A_WIDTH ** -0.5),
        "b_mu_rkv": jax.random.uniform(ks[8], (L, 3, B_WIDTH), jnp.float32),
        "b_mu_w": jax.random.uniform(ks[9], (L, DECAY_LORA), jnp.float32),
        "b_mu_a": jax.random.uniform(ks[10], (L, ICLR_LORA), jnp.float32),
        "b_w0": jax.random.uniform(ks[11], (L, B_WIDTH), jnp.float32, minval=-3.0, maxval=0.5),
        "b_w2": nrm(ks[12], (L, DECAY_LORA, B_WIDTH), 0.1),
        "b_a0": nrm(ks[13], (L, B_WIDTH), 0.5),
        "b_a2": nrm(ks[14], (L, ICLR_LORA, B_WIDTH), 0.1),
        "b_k_k": 0.85 + nrm(ks[15], (L, B_WIDTH), 0.05),
        "b_k_a": 1.0 + nrm(ks[16], (L, B_WIDTH), 0.05),
        "b_r_k": nrm(ks[17], (L, B_HEADS, B_HEAD_DIM), 0.1),
        "b_ln_w": 1.0 + nrm(ks[18], (L, B_WIDTH), 0.1),
        "b_ln_b": nrm(ks[19], (L, B_WIDTH), 0.02),
        "w_up_b": nrm(ks[20], (L, B_WIDTH, D_MODEL), B_WIDTH ** -0.5),
        "mem_norm_g": 1.0 + nrm(ks[21], (L, D_MODEL), 0.1),
        "w_mem_kv": nrm(ks[22], (L, D_MODEL, 2 * C_WIDTH), D_MODEL ** -0.5),
        "c_q_g": 1.0 + nrm(ks[23], (L, C_HEAD_DIM), 0.1),
        "c_k_g": 1.0 + nrm(ks[24], (L, C_HEAD_DIM), 0.1),
        "w_up_c": nrm(ks[25], (L, C_WIDTH, D_MODEL), C_WIDTH ** -0.5),
        "w_o": nrm(ks[26], (L, D_MODEL, D_MODEL), D_MODEL ** -0.5),
    }


def reference(x, mem, norm_g, w_in, a_q_g, a_k_g, a_rel_bias, w_up_a,
              b_mu_rkv, b_mu_w, b_mu_a, b_w0, b_w2, b_a0, b_a2, b_k_k, b_k_a, b_r_k,
              b_ln_w, b_ln_b, w_up_b, mem_norm_g, w_mem_kv, c_q_g, c_k_g, w_up_c, w_o):
    for l in range(DEPTH):
        x = hybrid_layer(x, mem, norm_g[l], w_in[l], a_q_g[l], a_k_g[l], a_rel_bias[l], w_up_a[l],
                         b_mu_rkv[l], b_mu_w[l], b_mu_a[l], b_w0[l], b_w2[l], b_a0[l], b_a2[l],
                         b_k_k[l], b_k_a[l], b_r_k[l], b_ln_w[l], b_ln_b[l], w_up_b[l],
                         mem_norm_g[l], w_mem_kv[l], c_q_g[l], c_k_g[l], w_up_c[l], w_o[l])
    return x
```

```python
import functools

import numpy as np
import jax
import jax.numpy as jnp
from jax import lax
from jax.experimental import pallas as pl
from jax.experimental.pallas import tpu as pltpu

D_MODEL = 2048
CHUNK = 64
N_MEM = 256
NORM_EPS = 1e-6
NEG_INF = -1e30

A_HEADS = 8
A_HEAD_DIM = 128
A_WIDTH = A_HEADS * A_HEAD_DIM
A_LEFT_CHUNKS = 8
REL_CLIP = 128

B_HEADS = 16
B_HEAD_DIM = 64
B_WIDTH = B_HEADS * B_HEAD_DIM
LORA = 64
GN_EPS = 64e-5

C_HEADS = 4
C_HEAD_DIM = 256
C_WIDTH = C_HEADS * C_HEAD_DIM

LANES = 128
VMEM_LIMIT = 56 * 1024 * 1024

Q_GROUP = 4 * CHUNK
K_WINDOW = A_LEFT_CHUNKS * CHUNK + Q_GROUP
K_PAD = A_LEFT_CHUNKS * CHUNK

COL_AQ, COL_AK, COL_AV, COL_AZ = 0, 1, 2, 3
COL_BR, COL_BK, COL_BV, COL_BZ = 4, 5, 6, 7
COL_CQ, COL_CZ = 8, 9
COL_GA, COL_GB, COL_GC = 10, 12, 14
MAIN_COLS = 16 * 1024

BF16 = jnp.bfloat16
F32 = jnp.float32


def _bdot(a, b):
    return jnp.dot(a.astype(BF16), b.astype(BF16), preferred_element_type=F32)


def _bdot_nt(a, b):
    return lax.dot_general(a.astype(BF16), b.astype(BF16), (((1,), (1,)), ((), ())),
                           preferred_element_type=F32)


def _bdot_tn(a, b):
    return lax.dot_general(a.astype(BF16), b.astype(BF16), (((0,), (0,)), ((), ())),
                           preferred_element_type=F32)


def _split2(x):
    hi = x.astype(BF16)
    lo = (x - hi.astype(F32)).astype(BF16)
    return hi, lo


def _split3(x):
    hi = x.astype(BF16)
    r1 = x - hi.astype(F32)
    mid = r1.astype(BF16)
    lo = (r1 - mid.astype(F32)).astype(BF16)
    return hi, mid, lo


def _silu(x):
    return x * jax.nn.sigmoid(x)


def _norm_matmul_kernel(x_ref, g_ref, w_ref, o_ref, h_ref):
    @pl.when(pl.program_id(1) == 0)
    def _():
        x = x_ref[...]
        ms = jnp.mean(x * x, axis=-1, keepdims=True)
        h_ref[...] = (x * lax.rsqrt(ms + NORM_EPS) * g_ref[...]).astype(BF16)

    o_ref[...] = jnp.dot(h_ref[...], w_ref[...], preferred_element_type=F32).astype(o_ref.dtype)


def _norm_matmul(x2d, g, w_bf16, tm, tn, name):
    m, k = x2d.shape
    n = w_bf16.shape[1]
    return pl.pallas_call(
        _norm_matmul_kernel,
        grid=(m // tm, n // tn),
        in_specs=[
            pl.BlockSpec((tm, k), lambda i, j: (i, 0)),
            pl.BlockSpec((1, k), lambda i, j: (0, 0)),
            pl.BlockSpec((k, tn), lambda i, j: (0, j)),
        ],
        out_specs=pl.BlockSpec((tm, tn), lambda i, j: (i, j)),
        out_shape=jax.ShapeDtypeStruct((m, n), F32),
        scratch_shapes=[pltpu.VMEM((tm, k), BF16)],
        compiler_params=pltpu.CompilerParams(
            dimension_semantics=("parallel", "arbitrary"), vmem_limit_bytes=VMEM_LIMIT),
        name=name,
    )(x2d, g.reshape(1, k), w_bf16)


def _band_attn_kernel(q_ref, k_ref, v_ref, z_ref, bias_ref, gq_ref, gk_ref, o_ref,
                      qn_ref, kp_ref, vp_ref):
    seq = q_ref.shape[0]
    scale = A_HEAD_DIM ** -0.5
    q = q_ref[...]
    qn = q * lax.rsqrt(jnp.mean(q * q, axis=-1, keepdims=True) + NORM_EPS) * gq_ref[...]
    qn_ref[...] = (qn * scale).astype(BF16)
    k = k_ref[...]
    kn = k * lax.rsqrt(jnp.mean(k * k, axis=-1, keepdims=True) + NORM_EPS) * gk_ref[...]
    kp_ref[0:K_PAD, :] = jnp.zeros((K_PAD, A_HEAD_DIM), BF16)
    vp_ref[0:K_PAD, :] = jnp.zeros((K_PAD, A_HEAD_DIM), BF16)
    kp_ref[K_PAD:, :] = kn.astype(BF16)
    vp_ref[K_PAD:, :] = v_ref[...].astype(BF16)
    bias = bias_ref[...]
    col = lax.broadcasted_iota(jnp.int32, (Q_GROUP, K_WINDOW), 1)
    for g in range(seq // Q_GROUP):
        q0 = g * Q_GROUP
        s = _bdot_nt(qn_ref[q0:q0 + Q_GROUP, :], kp_ref[q0:q0 + K_WINDOW, :]) + bias
        if q0 < K_PAD:
            s = jnp.where(col >= K_PAD - q0, s, NEG_INF)
        m = jnp.max(s, axis=-1, keepdims=True)
        p = jnp.exp(s - m)
        l = jnp.sum(p, axis=-1, keepdims=True)
        o = _bdot(p, vp_ref[q0:q0 + K_WINDOW, :]) / l
        o_ref[q0:q0 + Q_GROUP, :] = (o * _silu(z_ref[q0:q0 + Q_GROUP, :])).astype(o_ref.dtype)


def _band_bias_table(rel_bias):
    r = np.arange(Q_GROUP)[:, None]
    m = np.arange(K_WINDOW)[None, :]
    idx = np.clip(r + K_PAD - m, -REL_CLIP, REL_CLIP) + REL_CLIP
    lo = (r // CHUNK) * CHUNK
    band = (m >= lo) & (m < lo + (A_LEFT_CHUNKS + 1) * CHUNK)
    return jnp.where(jnp.asarray(band)[None], rel_bias[:, idx].astype(F32), NEG_INF)


def _band_attn(proj3, bias_tab, gq, gk):
    bsz, seq, _ = proj3.shape
    hb = A_WIDTH // A_HEAD_DIM

    def col(group):
        return pl.BlockSpec((None, seq, A_HEAD_DIM), lambda b, h: (b, 0, group * hb + h))

    return pl.pallas_call(
        _band_attn_kernel,
        grid=(bsz, A_HEADS),
        in_specs=[
            col(COL_AQ), col(COL_AK), col(COL_AV), col(COL_AZ),
            pl.BlockSpec((None, Q_GROUP, K_WINDOW), lambda b, h: (h, 0, 0)),
            pl.BlockSpec((1, A_HEAD_DIM), lambda b, h: (0, 0)),
            pl.BlockSpec((1, A_HEAD_DIM), lambda b, h: (0, 0)),
        ],
        out_specs=pl.BlockSpec((None, seq, A_HEAD_DIM), lambda b, h: (b, 0, h)),
        out_shape=jax.ShapeDtypeStruct((bsz, seq, A_WIDTH), BF16),
        scratch_shapes=[
            pltpu.VMEM((seq, A_HEAD_DIM), BF16),
            pltpu.VMEM((seq + K_PAD, A_HEAD_DIM), BF16),
            pltpu.VMEM((seq + K_PAD, A_HEAD_DIM), BF16),
        ],
        compiler_params=pltpu.CompilerParams(
            dimension_semantics=("parallel", "parallel"), vmem_limit_bytes=VMEM_LIMIT),
        name="band_attn",
    )(proj3, proj3, proj3, proj3, bias_tab, gq.reshape(1, -1), gk.reshape(1, -1))


def _mem_attn_kernel(q_ref, z_ref, mk_ref, mv_ref, gq_ref, gk_ref, o_ref):
    scale = C_HEAD_DIM ** -0.5
    for h in range(C_HEADS):
        sl = slice(h * C_HEAD_DIM, (h + 1) * C_HEAD_DIM)
        q = q_ref[:, sl]
        qn = q * lax.rsqrt(jnp.mean(q * q, axis=-1, keepdims=True) + NORM_EPS) * gq_ref[...] * scale
        k = mk_ref[:, sl]
        kn = k * lax.rsqrt(jnp.mean(k * k, axis=-1, keepdims=True) + NORM_EPS) * gk_ref[...]
        s = _bdot_nt(qn, kn)
        m = jnp.max(s, axis=-1, keepdims=True)
        p = jnp.exp(s - m)
        l = jnp.sum(p, axis=-1, keepdims=True)
        o = _bdot(p, mv_ref[:, sl]) / l
        o_ref[:, sl] = (o * _silu(z_ref[:, sl])).astype(o_ref.dtype)


def _mem_attn(proj3, mkv3, gq, gk, ts):
    bsz, seq, _ = proj3.shape
    return pl.pallas_call(
        _mem_attn_kernel,
        grid=(bsz, seq // ts),
        in_specs=[
            pl.BlockSpec((None, ts, C_WIDTH), lambda b, s: (b, s, COL_CQ)),
            pl.BlockSpec((None, ts, C_WIDTH), lambda b, s: (b, s, COL_CZ)),
            pl.BlockSpec((None, N_MEM, C_WIDTH), lambda b, s: (b, 0, 0)),
            pl.BlockSpec((None, N_MEM, C_WIDTH), lambda b, s: (b, 0, 1)),
            pl.BlockSpec((1, C_HEAD_DIM), lambda b, s: (0, 0)),
            pl.BlockSpec((1, C_HEAD_DIM), lambda b, s: (0, 0)),
        ],
        out_specs=pl.BlockSpec((None, ts, C_WIDTH), lambda b, s: (b, s, 0)),
        out_shape=jax.ShapeDtypeStruct((bsz, seq, C_WIDTH), BF16),
        compiler_params=pltpu.CompilerParams(
            dimension_semantics=("parallel", "parallel"), vmem_limit_bytes=VMEM_LIMIT),
        name="mem_attn",
    )(proj3, proj3, mkv3, mkv3, gq.reshape(1, -1), gk.reshape(1, -1))


def _shift_rows(x, carry_row):
    rolled = pltpu.roll(x, 1, 0)
    row = lax.broadcasted_iota(jnp.int32, x.shape, 0)
    return jnp.where(row == 0, carry_row, rolled)


def _seg_sum(x, seg_ones):
    hi, lo = _split2(x)
    return (jnp.dot(hi, seg_ones, preferred_element_type=F32)
            + jnp.dot(lo, seg_ones, preferred_element_type=F32))


def _rwkv_kernel(pr_ref, pk_ref, pv_ref, pz_ref, la_ref,
                 mur_ref, muk_ref, muv_ref, mula_ref, w0_ref, a0_ref, w2a2_ref,
                 kk_ref, ka_ref, rk_ref, lnw_ref, lnb_ref,
                 o_ref,
                 st_ref, cr_ref, ck_ref, cv_ref, cla_ref):
    L = pr_ref.shape[0]
    c = pl.program_id(1)

    @pl.when(c == 0)
    def _():
        st_ref[...] = jnp.zeros_like(st_ref)
        cr_ref[...] = jnp.zeros_like(cr_ref)
        ck_ref[...] = jnp.zeros_like(ck_ref)
        cv_ref[...] = jnp.zeros_like(cv_ref)
        cla_ref[...] = jnp.zeros_like(cla_ref)

    def lerp(x_ref, carry_ref, mu_ref):
        x = x_ref[...]
        prev = _shift_rows(x, carry_ref[0:1, :])
        carry_ref[0:1, :] = x[L - 1:L, :]
        return x + mu_ref[...] * (prev - x)

    r = lerp(pr_ref, cr_ref, mur_ref)
    k = lerp(pk_ref, ck_ref, muk_ref)
    v = lerp(pv_ref, cv_ref, muv_ref)
    la = lerp(la_ref, cla_ref, mula_ref)

    lane = lax.broadcasted_iota(jnp.int32, (L, LANES), 1)
    lo_half = lane < B_HEAD_DIM
    xw = jnp.where(lo_half, jnp.tanh(la), 0.0)
    xa = jnp.where(lo_half, 0.0, la)
    w_hi, w_lo = _split2(w2a2_ref[...])

    def dot3(x):
        x_hi, x_lo = _split2(x)
        return (jnp.dot(x_hi, w_hi, preferred_element_type=F32)
                + jnp.dot(x_lo, w_hi, preferred_element_type=F32)
                + jnp.dot(x_hi, w_lo, preferred_element_type=F32))

    zw = -(w0_ref[...] + dot3(xw))
    softplus = jnp.maximum(zw, 0.0) + jnp.log(1.0 + jnp.exp(-jnp.abs(zw)))
    lw = -jnp.exp(-softplus - 0.5)
    a_sig = jax.nn.sigmoid(a0_ref[...] + dot3(xa))

    ti = lax.broadcasted_iota(jnp.int32, (L, L), 0)
    si = lax.broadcasted_iota(jnp.int32, (L, L), 1)
    tri = (si <= ti).astype(BF16)
    l_hi, l_mid, l_lo = _split3(lw)
    cum = (jnp.dot(tri, l_hi, preferred_element_type=F32)
           + jnp.dot(tri, l_mid, preferred_element_type=F32)
           + jnp.dot(tri, l_lo, preferred_element_type=F32))
    e_pos = jnp.exp(cum)
    e_neg = jnp.exp(-cum)
    e_prev = jnp.exp(cum - lw)

    strict = si < ti
    incl = si <= ti
    ri = lax.broadcasted_iota(jnp.int32, (LANES, LANES), 0)
    ci = lax.broadcasted_iota(jnp.int32, (LANES, LANES), 1)
    same_head = (ri < B_HEAD_DIM) == (ci < B_HEAD_DIM)
    seg_ones = same_head.astype(BF16)
    inv_n = 1.0 / B_HEAD_DIM

    for p in range(B_HEADS // 2):
        sl = slice(p * LANES, (p + 1) * LANES)
        r2, k2, v2 = r[:, sl], k[:, sl], v[:, sl]
        asig2 = a_sig[:, sl]
        kk2 = k2 * kk_ref[:, sl]
        kk2 = kk2 * lax.rsqrt(jnp.maximum(_seg_sum(kk2 * kk2, seg_ones), 1e-24))
        kmod2 = k2 * (1.0 + (asig2 - 1.0) * ka_ref[:, sl])
        rt2 = r2 * e_pos[:, sl]
        kt2 = kmod2 * e_neg[:, sl]
        bt2 = (kk2 * asig2) * e_neg[:, sl]
        at2 = (-kk2) * e_prev[:, sl]
        decay_all = e_pos[L - 1:L, sl]
        st = st_ref[p]

        wa2 = jnp.zeros((L, LANES), F32)
        uv2 = jnp.zeros((L, LANES), F32)
        a_rb, a_rk = [], []
        for e in range(2):
            me = lo_half if e == 0 else jnp.logical_not(lo_half)
            at_e = jnp.where(me, at2, 0.0)
            ar_e = jnp.concatenate([at_e, jnp.where(me, rt2, 0.0)], axis=0)
            sb = _bdot_nt(ar_e, bt2)
            sk = _bdot_nt(ar_e, kt2)
            a_ab = jnp.where(strict, sb[:L], 0.0)
            a_ak = jnp.where(strict, sk[:L], 0.0)
            a_rb.append(jnp.where(incl, sb[L:], 0.0))
            a_rk.append(jnp.where(incl, sk[L:], 0.0))
            x = jnp.concatenate([at_e, jnp.where(me, _bdot(a_ak, v2), 0.0)], axis=1)
            pw = a_ab
            n = 1
            while n < L:
                x = x + _bdot(pw, x)
                n *= 2
                if n < L:
                    pw = _bdot(pw, pw)
            wa2 = wa2 + x[:, :LANES]
            uv2 = uv2 + x[:, LANES:]

        hs = _bdot_nt(jnp.concatenate([wa2, rt2], axis=0), st)
        u2 = hs[:L] + uv2
        o2 = hs[L:] + jnp.where(lo_half,
                                _bdot(a_rb[0], u2) + _bdot(a_rk[0], v2),
                                _bdot(a_rb[1], u2) + _bdot(a_rk[1], v2))
        upd = _bdot_tn(jnp.concatenate([u2, v2], axis=0), jnp.concatenate([bt2, kt2], axis=0))
        st_ref[p] = (st + jnp.where(same_head, upd, 0.0)) * decay_all

        mean = _seg_sum(o2, seg_ones) * inv_n
        d = o2 - mean
        var = _seg_sum(d * d, seg_ones) * inv_n
        on = d * lax.rsqrt(var + GN_EPS) * lnw_ref[:, sl] + lnb_ref[:, sl]
        bonus = _seg_sum(r2 * kmod2 * rk_ref[:, sl], seg_ones) * v2
        o_ref[:, sl] = ((on + bonus) * _silu(pz_ref[:, sl])).astype(o_ref.dtype)


def _rwkv(proj3, lora3, mu_rkv, mu_la, w0, a0, w2a2, k_k, k_a, r_k, ln_w, ln_b):
    bsz, seq, _ = proj3.shape
    L = CHUNK

    def col(group):
        return pl.BlockSpec((None, L, B_WIDTH), lambda b, c: (b, c, group))

    def vec(n):
        return pl.BlockSpec((1, n), lambda b, c: (0, 0))

    row = lambda t: t.reshape(1, -1)
    return pl.pallas_call(
        _rwkv_kernel,
        grid=(bsz, seq // L),
        in_specs=[
            col(COL_BR), col(COL_BK), col(COL_BV), col(COL_BZ),
            pl.BlockSpec((None, L, 2 * LORA), lambda b, c: (b, c, 0)),
            vec(B_WIDTH), vec(B_WIDTH), vec(B_WIDTH), vec(2 * LORA), vec(B_WIDTH), vec(B_WIDTH),
            pl.BlockSpec((2 * LORA, B_WIDTH), lambda b, c: (0, 0)),
            vec(B_WIDTH), vec(B_WIDTH), vec(B_WIDTH), vec(B_WIDTH), vec(B_WIDTH),
        ],
        out_specs=pl.BlockSpec((None, L, B_WIDTH), lambda b, c: (b, c, 0)),
        out_shape=jax.ShapeDtypeStruct((bsz, seq, B_WIDTH), BF16),
        scratch_shapes=[
            pltpu.VMEM((B_HEADS // 2, LANES, LANES), F32),
            pltpu.VMEM((8, B_WIDTH), F32), pltpu.VMEM((8, B_WIDTH), F32), pltpu.VMEM((8, B_WIDTH), F32),
            pltpu.VMEM((8, 2 * LORA), F32),
        ],
        compiler_params=pltpu.CompilerParams(
            dimension_semantics=("parallel", "arbitrary"), vmem_limit_bytes=VMEM_LIMIT),
        name="rwkv",
    )(proj3, proj3, proj3, proj3, lora3,
      row(mu_rkv[0]), row(mu_rkv[1]), row(mu_rkv[2]), row(mu_la), row(w0), row(a0), w2a2,
      row(k_k), row(k_a), row(r_k), row(ln_w), row(ln_b))


def _merge_out_kernel(ya_ref, yb_ref, yc_ref, ga_ref, gb_ref, gc_ref, wa_ref, wb_ref, wc_ref, wo_ref, x_ref,
                      o_ref):
    n = pl.program_id(1)

    @pl.when(n == 0)
    def _():
        o_ref[...] = x_ref[...]

    merged = (jax.nn.sigmoid(ga_ref[...]) * jnp.dot(ya_ref[...], wa_ref[...], preferred_element_type=F32)
              + jax.nn.sigmoid(gb_ref[...]) * jnp.dot(yb_ref[...], wb_ref[...], preferred_element_type=F32)
              + jax.nn.sigmoid(gc_ref[...]) * jnp.dot(yc_ref[...], wc_ref[...], preferred_element_type=F32))
    o_ref[...] += jnp.dot(merged.astype(BF16), wo_ref[...], preferred_element_type=F32)


def _merge_out(ya, yb, yc, proj2, wa, wb, wc, wo, x2d, tm, tn):
    m = x2d.shape[0]
    gpb = 1024 // tn

    def ybranch(width):
        return pl.BlockSpec((tm, width), lambda i, n: (i, 0))

    def gate(group):
        return pl.BlockSpec((tm, tn), lambda i, n: (i, group * gpb + n))

    def wup(width):
        return pl.BlockSpec((width, tn), lambda i, n: (0, n))

    return pl.pallas_call(
        _merge_out_kernel,
        grid=(m // tm, D_MODEL // tn),
        in_specs=[
            ybranch(A_WIDTH), ybranch(B_WIDTH), ybranch(C_WIDTH),
            gate(COL_GA), gate(COL_GB), gate(COL_GC),
            wup(A_WIDTH), wup(B_WIDTH), wup(C_WIDTH),
            pl.BlockSpec((tn, D_MODEL), lambda i, n: (n, 0)),
            pl.BlockSpec((tm, D_MODEL), lambda i, n: (i, 0)),
        ],
        out_specs=pl.BlockSpec((tm, D_MODEL), lambda i, n: (i, 0)),
        out_shape=jax.ShapeDtypeStruct((m, D_MODEL), F32),
        compiler_params=pltpu.CompilerParams(
            dimension_semantics=("parallel", "arbitrary"), vmem_limit_bytes=VMEM_LIMIT),
        name="merge_out",
    )(ya, yb, yc, proj2, proj2, proj2, wa, wb, wc, wo, x2d)


def _layer(x, mem, norm_g, w_in, a_q_g, a_k_g, a_rel_bias, w_up_a,
           b_mu_rkv, b_mu_w, b_mu_a, b_w0, b_w2, b_a0, b_a2, b_k_k, b_k_a, b_r_k,
           b_ln_w, b_ln_b, w_up_b, mem_norm_g, w_mem_kv, c_q_g, c_k_g, w_up_c, w_o):
    bsz, seq, d = x.shape
    t = bsz * seq
    x2d = x.reshape(t, d)
    lora_lo = 4 * A_WIDTH + 4 * B_WIDTH
    lora_hi = lora_lo + 2 * LORA
    w_main = jnp.concatenate([w_in[:, :lora_lo], w_in[:, lora_hi:]], axis=1).astype(BF16)
    w_lora = w_in[:, lora_lo:lora_hi].astype(BF16)

    proj = _norm_matmul(x2d, norm_g, w_main, 1024, 1024, "in_proj")
    lora = _norm_matmul(x2d, norm_g, w_lora, 1024, 2 * LORA, "in_proj_lora")
    mkv = _norm_matmul(mem.reshape(bsz * N_MEM, d), mem_norm_g, w_mem_kv.astype(BF16), 1024, 1024, "mem_kv")

    proj3 = proj.reshape(bsz, seq, MAIN_COLS)
    ya = _band_attn(proj3, _band_bias_table(a_rel_bias), a_q_g, a_k_g)
    yc = _mem_attn(proj3, mkv.reshape(bsz, N_MEM, 2 * C_WIDTH), c_q_g, c_k_g, 512)
    yb = _rwkv(proj3, lora.reshape(bsz, seq, 2 * LORA), b_mu_rkv,
               jnp.concatenate([b_mu_w, b_mu_a]), b_w0, b_a0,
               jnp.concatenate([b_w2, b_a2], axis=0), b_k_k, b_k_a, b_r_k.reshape(-1), b_ln_w, b_ln_b)

    out = _merge_out(ya.reshape(t, A_WIDTH), yb.reshape(t, B_WIDTH), yc.reshape(t, C_WIDTH), proj,
                     w_up_a.astype(BF16), w_up_b.astype(BF16), w_up_c.astype(BF16), w_o.astype(BF16),
                     x2d, 512, 512)
    return out.reshape(bsz, seq, d)


def kernel(x, mem, norm_g, w_in, a_q_g, a_k_g, a_rel_bias, w_up_a, b_mu_rkv, b_mu_w, b_mu_a, b_w0, b_w2, b_a0, b_a2, b_k_k, b_k_a, b_r_k, b_ln_w, b_ln_b, w_up_b, mem_norm_g, w_mem_kv, c_q_g, c_k_g, w_up_c, w_o):
    for l in range(norm_g.shape[0]):
        x = _layer(x, mem, norm_g[l], w_in[l], a_q_g[l], a_k_g[l], a_rel_bias[l], w_up_a[l],
                   b_mu_rkv[l], b_mu_w[l], b_mu_a[l], b_w0[l], b_w2[l], b_a0[l], b_a2[l],
                   b_k_k[l], b_k_a[l], b_r_k[l], b_ln_w[l], b_ln_b[l], w_up_b[l],
                   mem_norm_g[l], w_mem_kv[l], c_q_g[l], c_k_g[l], w_up_c[l], w_o[l])
    return x
```

```python
import functools

import numpy as np
import jax
import jax.numpy as jnp
from jax import lax
from jax.experimental import pallas as pl
from jax.experimental.pallas import tpu as pltpu

D_MODEL = 2048
CHUNK = 64
N_MEM = 256
NORM_EPS = 1e-6
NEG_INF = -1e30

A_HEADS = 8
A_HEAD_DIM = 128
A_WIDTH = A_HEADS * A_HEAD_DIM
A_LEFT_CHUNKS = 8
REL_CLIP = 128

B_HEADS = 16
B_HEAD_DIM = 64
B_WIDTH = B_HEADS * B_HEAD_DIM
LORA = 64
GN_EPS = 64e-5

C_HEADS = 4
C_HEAD_DIM = 256
C_WIDTH = C_HEADS * C_HEAD_DIM

LANES = 128
VMEM_LIMIT = 56 * 1024 * 1024

Q_GROUP = 4 * CHUNK
K_WINDOW = A_LEFT_CHUNKS * CHUNK + Q_GROUP
K_PAD = A_LEFT_CHUNKS * CHUNK

COL_AQ, COL_AK, COL_AV, COL_AZ = 0, 1, 2, 3
COL_BR, COL_BK, COL_BV, COL_BZ = 4, 5, 6, 7
COL_CQ, COL_CZ = 8, 9
COL_GA, COL_GB, COL_GC = 10, 12, 14
MAIN_COLS = 16 * 1024

BF16 = jnp.bfloat16
F32 = jnp.float32


def _bdot(a, b):
    return jnp.dot(a.astype(BF16), b.astype(BF16), preferred_element_type=F32)


def _bdot_nt(a, b):
    return lax.dot_general(a.astype(BF16), b.astype(BF16), (((1,), (1,)), ((), ())),
                           preferred_element_type=F32)


def _bdot_tn(a, b):
    return lax.dot_general(a.astype(BF16), b.astype(BF16), (((0,), (0,)), ((), ())),
                           preferred_element_type=F32)


def _split2(x):
    hi = x.astype(BF16)
    lo = (x - hi.astype(F32)).astype(BF16)
    return hi, lo


def _split3(x):
    hi = x.astype(BF16)
    r1 = x - hi.astype(F32)
    mid = r1.astype(BF16)
    lo = (r1 - mid.astype(F32)).astype(BF16)
    return hi, mid, lo


def _silu(x):
    return x * jax.nn.sigmoid(x)


def _norm_matmul_kernel(x_ref, g_ref, w_ref, o_ref, h_ref):
    @pl.when(pl.program_id(1) == 0)
    def _():
        x = x_ref[...]
        ms = jnp.mean(x * x, axis=-1, keepdims=True)
        h_ref[...] = (x * lax.rsqrt(ms + NORM_EPS) * g_ref[...]).astype(BF16)

    o_ref[...] = jnp.dot(h_ref[...], w_ref[...], preferred_element_type=F32).astype(o_ref.dtype)


def _norm_matmul(x2d, g, w_bf16, tm, tn, name):
    m, k = x2d.shape
    n = w_bf16.shape[1]
    return pl.pallas_call(
        _norm_matmul_kernel,
        grid=(m // tm, n // tn),
        in_specs=[
            pl.BlockSpec((tm, k), lambda i, j: (i, 0)),
            pl.BlockSpec((1, k), lambda i, j: (0, 0)),
            pl.BlockSpec((k, tn), lambda i, j: (0, j)),
        ],
        out_specs=pl.BlockSpec((tm, tn), lambda i, j: (i, j)),
        out_shape=jax.ShapeDtypeStruct((m, n), F32),
        scratch_shapes=[pltpu.VMEM((tm, k), BF16)],
        compiler_params=pltpu.CompilerParams(
            dimension_semantics=("parallel", "arbitrary"), vmem_limit_bytes=VMEM_LIMIT),
        name=name,
    )(x2d, g.reshape(1, k), w_bf16)


def _band_attn_kernel(q_ref, k_ref, v_ref, z_ref, bias_ref, gq_ref, gk_ref, o_ref,
                      qn_ref, kp_ref, vp_ref):
    seq = q_ref.shape[0]
    scale = A_HEAD_DIM ** -0.5
    q = q_ref[...]
    qn = q * lax.rsqrt(jnp.mean(q * q, axis=-1, keepdims=True) + NORM_EPS) * gq_ref[...]
    qn_ref[...] = (qn * scale).astype(BF16)
    k = k_ref[...]
    kn = k * lax.rsqrt(jnp.mean(k * k, axis=-1, keepdims=True) + NORM_EPS) * gk_ref[...]
    kp_ref[0:K_PAD, :] = jnp.zeros((K_PAD, A_HEAD_DIM), BF16)
    vp_ref[0:K_PAD, :] = jnp.zeros((K_PAD, A_HEAD_DIM), BF16)
    kp_ref[K_PAD:, :] = kn.astype(BF16)
    vp_ref[K_PAD:, :] = v_ref[...].astype(BF16)
    bias = bias_ref[...]
    col = lax.broadcasted_iota(jnp.int32, (Q_GROUP, K_WINDOW), 1)
    for g in range(seq // Q_GROUP):
        q0 = g * Q_GROUP
        s = _bdot_nt(qn_ref[q0:q0 + Q_GROUP, :], kp_ref[q0:q0 + K_WINDOW, :]) + bias
        if q0 < K_PAD:
            s = jnp.where(col >= K_PAD - q0, s, NEG_INF)
        m = jnp.max(s, axis=-1, keepdims=True)
        p = jnp.exp(s - m)
        l = jnp.sum(p, axis=-1, keepdims=True)
        o = _bdot(p, vp_ref[q0:q0 + K_WINDOW, :]) / l
        o_ref[q0:q0 + Q_GROUP, :] = (o * _silu(z_ref[q0:q0 + Q_GROUP, :])).astype(o_ref.dtype)


def _band_bias_table(rel_bias):
    period = 1024
    q = np.arange(period)
    dist = (Q_GROUP - 1 - q) + K_PAD
    f = rel_bias[:, np.clip(dist, -REL_CLIP, REL_CLIP) + REL_CLIP].astype(F32)
    seq = jnp.tile(f, (1, Q_GROUP + 1))[:, :Q_GROUP * (period + 1)]
    toeplitz = seq.reshape(-1, Q_GROUP, period + 1)[:, ::-1, :K_WINDOW]
    r = np.arange(Q_GROUP)[:, None]
    m = np.arange(K_WINDOW)[None, :]
    lo = (r // CHUNK) * CHUNK
    band = (m >= lo) & (m < lo + (A_LEFT_CHUNKS + 1) * CHUNK)
    return jnp.where(jnp.asarray(band)[None], toeplitz, NEG_INF)


def _band_attn(proj3, bias_tab, gq, gk):
    bsz, seq, _ = proj3.shape
    hb = A_WIDTH // A_HEAD_DIM

    def col(group):
        return pl.BlockSpec((None, seq, A_HEAD_DIM), lambda b, h: (b, 0, group * hb + h))

    return pl.pallas_call(
        _band_attn_kernel,
        grid=(bsz, A_HEADS),
        in_specs=[
            col(COL_AQ), col(COL_AK), col(COL_AV), col(COL_AZ),
            pl.BlockSpec((None, Q_GROUP, K_WINDOW), lambda b, h: (h, 0, 0)),
            pl.BlockSpec((1, A_HEAD_DIM), lambda b, h: (0, 0)),
            pl.BlockSpec((1, A_HEAD_DIM), lambda b, h: (0, 0)),
        ],
        out_specs=pl.BlockSpec((None, seq, A_HEAD_DIM), lambda b, h: (b, 0, h)),
        out_shape=jax.ShapeDtypeStruct((bsz, seq, A_WIDTH), BF16),
        scratch_shapes=[
            pltpu.VMEM((seq, A_HEAD_DIM), BF16),
            pltpu.VMEM((seq + K_PAD, A_HEAD_DIM), BF16),
            pltpu.VMEM((seq + K_PAD, A_HEAD_DIM), BF16),
        ],
        compiler_params=pltpu.CompilerParams(
            dimension_semantics=("parallel", "parallel"), vmem_limit_bytes=VMEM_LIMIT),
        name="band_attn",
    )(proj3, proj3, proj3, proj3, bias_tab, gq.reshape(1, -1), gk.reshape(1, -1))


def _mem_attn_kernel(q_ref, z_ref, mk_ref, mv_ref, gq_ref, gk_ref, o_ref):
    scale = C_HEAD_DIM ** -0.5
    for h in range(C_HEADS):
        sl = slice(h * C_HEAD_DIM, (h + 1) * C_HEAD_DIM)
        q = q_ref[:, sl]
        qn = q * lax.rsqrt(jnp.mean(q * q, axis=-1, keepdims=True) + NORM_EPS) * gq_ref[...] * scale
        k = mk_ref[:, sl]
        kn = k * lax.rsqrt(jnp.mean(k * k, axis=-1, keepdims=True) + NORM_EPS) * gk_ref[...]
        s = _bdot_nt(qn, kn)
        m = jnp.max(s, axis=-1, keepdims=True)
        p = jnp.exp(s - m)
        l = jnp.sum(p, axis=-1, keepdims=True)
        o = _bdot(p, mv_ref[:, sl]) / l
        o_ref[:, sl] = (o * _silu(z_ref[:, sl])).astype(o_ref.dtype)


def _mem_attn(proj3, mkv3, gq, gk, ts):
    bsz, seq, _ = proj3.shape
    return pl.pallas_call(
        _mem_attn_kernel,
        grid=(bsz, seq // ts),
        in_specs=[
            pl.BlockSpec((None, ts, C_WIDTH), lambda b, s: (b, s, COL_CQ)),
            pl.BlockSpec((None, ts, C_WIDTH), lambda b, s: (b, s, COL_CZ)),
            pl.BlockSpec((None, N_MEM, C_WIDTH), lambda b, s: (b, 0, 0)),
            pl.BlockSpec((None, N_MEM, C_WIDTH), lambda b, s: (b, 0, 1)),
            pl.BlockSpec((1, C_HEAD_DIM), lambda b, s: (0, 0)),
            pl.BlockSpec((1, C_HEAD_DIM), lambda b, s: (0, 0)),
        ],
        out_specs=pl.BlockSpec((None, ts, C_WIDTH), lambda b, s: (b, s, 0)),
        out_shape=jax.ShapeDtypeStruct((bsz, seq, C_WIDTH), BF16),
        compiler_params=pltpu.CompilerParams(
            dimension_semantics=("parallel", "parallel"), vmem_limit_bytes=VMEM_LIMIT),
        name="mem_attn",
    )(proj3, proj3, mkv3, mkv3, gq.reshape(1, -1), gk.reshape(1, -1))


def _shift_rows(x, carry_row):
    rolled = pltpu.roll(x, 1, 0)
    row = lax.broadcasted_iota(jnp.int32, x.shape, 0)
    return jnp.where(row == 0, carry_row, rolled)


def _seg_sum(x, seg_ones):
    hi, lo = _split2(x)
    return (jnp.dot(hi, seg_ones, preferred_element_type=F32)
            + jnp.dot(lo, seg_ones, preferred_element_type=F32))


def _rwkv_kernel(pr_ref, pk_ref, pv_ref, pz_ref, la_ref,
                 mur_ref, muk_ref, muv_ref, mula_ref, w0_ref, a0_ref, w2a2_ref,
                 kk_ref, ka_ref, rk_ref, lnw_ref, lnb_ref,
                 o_ref,
                 st_ref, cr_ref, ck_ref, cv_ref, cla_ref):
    L = pr_ref.shape[0]
    c = pl.program_id(1)

    @pl.when(c == 0)
    def _():
        st_ref[...] = jnp.zeros_like(st_ref)
        cr_ref[...] = jnp.zeros_like(cr_ref)
        ck_ref[...] = jnp.zeros_like(ck_ref)
        cv_ref[...] = jnp.zeros_like(cv_ref)
        cla_ref[...] = jnp.zeros_like(cla_ref)

    def lerp(x_ref, carry_ref, mu_ref):
        x = x_ref[...]
        prev = _shift_rows(x, carry_ref[0:1, :])
        carry_ref[0:1, :] = x[L - 1:L, :]
        return x + mu_ref[...] * (prev - x)

    r = lerp(pr_ref, cr_ref, mur_ref)
    k = lerp(pk_ref, ck_ref, muk_ref)
    v = lerp(pv_ref, cv_ref, muv_ref)
    la = lerp(la_ref, cla_ref, mula_ref)

    lane = lax.broadcasted_iota(jnp.int32, (L, LANES), 1)
    lo_half = lane < B_HEAD_DIM
    xw = jnp.where(lo_half, jnp.tanh(la), 0.0)
    xa = jnp.where(lo_half, 0.0, la)
    w_hi, w_lo = _split2(w2a2_ref[...])

    def dot3(x):
        x_hi, x_lo = _split2(x)
        return (jnp.dot(x_hi, w_hi, preferred_element_type=F32)
                + jnp.dot(x_lo, w_hi, preferred_element_type=F32)
                + jnp.dot(x_hi, w_lo, preferred_element_type=F32))

    zw = -(w0_ref[...] + dot3(xw))
    softplus = jnp.maximum(zw, 0.0) + jnp.log(1.0 + jnp.exp(-jnp.abs(zw)))
    lw = -jnp.exp(-softplus - 0.5)
    a_sig = jax.nn.sigmoid(a0_ref[...] + dot3(xa))

    ti = lax.broadcasted_iota(jnp.int32, (L, L), 0)
    si = lax.broadcasted_iota(jnp.int32, (L, L), 1)
    tri = (si <= ti).astype(BF16)
    l_hi, l_mid, l_lo = _split3(lw)
    cum = (jnp.dot(tri, l_hi, preferred_element_type=F32)
           + jnp.dot(tri, l_mid, preferred_element_type=F32)
           + jnp.dot(tri, l_lo, preferred_element_type=F32))
    e_pos = jnp.exp(cum)
    e_neg = jnp.exp(-cum)
    e_prev = jnp.exp(cum - lw)

    ri = lax.broadcasted_iota(jnp.int32, (LANES, LANES), 0)
    ci = lax.broadcasted_iota(jnp.int32, (LANES, LANES), 1)
    seg_ones = ((ri < B_HEAD_DIM) == (ci < B_HEAD_DIM)).astype(BF16)
    inv_n = 1.0 / B_HEAD_DIM
    mi = lax.broadcasted_iota(jnp.int32, (4 * L, 4 * L), 0)
    mj = lax.broadcasted_iota(jnp.int32, (4 * L, 4 * L), 1)
    tt, ss = mi & (L - 1), mj & (L - 1)
    tri_mask = (ss < tt) | ((mi >= 2 * L) & (ss == tt))
    hi_half = jnp.logical_not(lo_half)

    def stack_heads(x2):
        return jnp.concatenate([jnp.where(lo_half, x2, 0.0), jnp.where(hi_half, x2, 0.0)], axis=0)

    pairs = range(B_HEADS // 2)
    sls = [slice(p * LANES, (p + 1) * LANES) for p in pairs]

    kk_raw = [k[:, sl] * kk_ref[:, sl] for sl in sls]
    kmod = [k[:, sl] * (1.0 + (a_sig[:, sl] - 1.0) * ka_ref[:, sl]) for sl in sls]
    sums = [_seg_sum(jnp.concatenate([kk_raw[p] * kk_raw[p], r[:, sls[p]] * kmod[p] * rk_ref[:, sls[p]]], axis=0),
                     seg_ones) for p in pairs]
    kk = [kk_raw[p] * lax.rsqrt(jnp.maximum(sums[p][:L], 1e-24)) for p in pairs]
    bonus = [sums[p][L:] * v[:, sls[p]] for p in pairs]
    vs = [stack_heads(v[:, sl]).astype(BF16) for sl in sls]
    rts = [stack_heads(r[:, sls[p]] * e_pos[:, sls[p]]) for p in pairs]
    ats = [stack_heads((-kk[p]) * e_prev[:, sls[p]]) for p in pairs]
    bks = [jnp.concatenate([stack_heads((kk[p] * a_sig[:, sls[p]]) * e_neg[:, sls[p]]),
                            stack_heads(kmod[p] * e_neg[:, sls[p]])], axis=0).astype(BF16) for p in pairs]

    sm = [jnp.where(tri_mask, _bdot_nt(jnp.concatenate([ats[p], rts[p]], axis=0), bks[p]), 0.0).astype(BF16)
          for p in pairs]
    av = [jnp.dot(sm[p][:, 2 * L:], vs[p], preferred_element_type=F32) for p in pairs]

    xs = [jnp.concatenate([ats[p], av[p][:2 * L]], axis=1) for p in pairs]
    pw = [sm[p][:2 * L, :2 * L] for p in pairs]
    n = 1
    while n < L:
        xs = [xs[p] + jnp.dot(pw[p], xs[p].astype(BF16), preferred_element_type=F32) for p in pairs]
        n *= 2
        if n < L:
            pw = [jnp.dot(pw[p], pw[p], preferred_element_type=F32).astype(BF16) for p in pairs]

    st = [st_ref[p] for p in pairs]
    hs = [_bdot_nt(jnp.concatenate([xs[p][:, :LANES], rts[p]], axis=0), st[p]) for p in pairs]
    us = [hs[p][:2 * L] + xs[p][:, LANES:] for p in pairs]
    osk = [hs[p][2 * L:] + jnp.dot(sm[p][2 * L:, :2 * L], us[p].astype(BF16), preferred_element_type=F32)
           + av[p][2 * L:] for p in pairs]
    for p in pairs:
        upd = _bdot_tn(jnp.concatenate([us[p].astype(BF16), vs[p]], axis=0), bks[p])
        st_ref[p] = (st[p] + upd) * e_pos[L - 1:L, sls[p]]

    o2 = [osk[p][:L] + osk[p][L:] for p in pairs]
    mean = [_seg_sum(o2[p], seg_ones) * inv_n for p in pairs]
    dev = [o2[p] - mean[p] for p in pairs]
    var = [_seg_sum(dev[p] * dev[p], seg_ones) * inv_n for p in pairs]
    for p in pairs:
        sl = sls[p]
        on = dev[p] * lax.rsqrt(var[p] + GN_EPS) * lnw_ref[:, sl] + lnb_ref[:, sl]
        o_ref[:, sl] = ((on + bonus[p]) * _silu(pz_ref[:, sl])).astype(o_ref.dtype)


def _rwkv(proj3, lora3, mu_rkv, mu_la, w0, a0, w2a2, k_k, k_a, r_k, ln_w, ln_b):
    bsz, seq, _ = proj3.shape
    L = CHUNK

    def col(group):
        return pl.BlockSpec((None, L, B_WIDTH), lambda b, c: (b, c, group))

    def vec(n):
        return pl.BlockSpec((1, n), lambda b, c: (0, 0))

    row = lambda t: t.reshape(1, -1)
    return pl.pallas_call(
        _rwkv_kernel,
        grid=(bsz, seq // L),
        in_specs=[
            col(COL_BR), col(COL_BK), col(COL_BV), col(COL_BZ),
            pl.BlockSpec((None, L, 2 * LORA), lambda b, c: (b, c, 0)),
            vec(B_WIDTH), vec(B_WIDTH), vec(B_WIDTH), vec(2 * LORA), vec(B_WIDTH), vec(B_WIDTH),
            pl.BlockSpec((2 * LORA, B_WIDTH), lambda b, c: (0, 0)),
            vec(B_WIDTH), vec(B_WIDTH), vec(B_WIDTH), vec(B_WIDTH), vec(B_WIDTH),
        ],
        out_specs=pl.BlockSpec((None, L, B_WIDTH), lambda b, c: (b, c, 0)),
        out_shape=jax.ShapeDtypeStruct((bsz, seq, B_WIDTH), BF16),
        scratch_shapes=[
            pltpu.VMEM((B_HEADS // 2, LANES, LANES), F32),
            pltpu.VMEM((8, B_WIDTH), F32), pltpu.VMEM((8, B_WIDTH), F32), pltpu.VMEM((8, B_WIDTH), F32),
            pltpu.VMEM((8, 2 * LORA), F32),
        ],
        compiler_params=pltpu.CompilerParams(
            dimension_semantics=("parallel", "arbitrary"), vmem_limit_bytes=VMEM_LIMIT),
        name="rwkv",
    )(proj3, proj3, proj3, proj3, lora3,
      row(mu_rkv[0]), row(mu_rkv[1]), row(mu_rkv[2]), row(mu_la), row(w0), row(a0), w2a2,
      row(k_k), row(k_a), row(r_k), row(ln_w), row(ln_b))


def _merge_out_kernel(ya_ref, yb_ref, yc_ref, ga_ref, gb_ref, gc_ref, wa_ref, wb_ref, wc_ref, wo_ref, x_ref,
                      o_ref):
    n = pl.program_id(1)

    @pl.when(n == 0)
    def _():
        o_ref[...] = x_ref[...]

    merged = (jax.nn.sigmoid(ga_ref[...]) * jnp.dot(ya_ref[...], wa_ref[...], preferred_element_type=F32)
              + jax.nn.sigmoid(gb_ref[...]) * jnp.dot(yb_ref[...], wb_ref[...], preferred_element_type=F32)
              + jax.nn.sigmoid(gc_ref[...]) * jnp.dot(yc_ref[...], wc_ref[...], preferred_element_type=F32))
    o_ref[...] += jnp.dot(merged.astype(BF16), wo_ref[...], preferred_element_type=F32)


def _merge_out(ya, yb, yc, proj2, wa, wb, wc, wo, x2d, tm, tn):
    m = x2d.shape[0]
    gpb = 1024 // tn

    def ybranch(width):
        return pl.BlockSpec((tm, width), lambda i, n: (i, 0))

    def gate(group):
        return pl.BlockSpec((tm, tn), lambda i, n: (i, group * gpb + n))

    def wup(width):
        return pl.BlockSpec((width, tn), lambda i, n: (0, n))

    return pl.pallas_call(
        _merge_out_kernel,
        grid=(m // tm, D_MODEL // tn),
        in_specs=[
            ybranch(A_WIDTH), ybranch(B_WIDTH), ybranch(C_WIDTH),
            gate(COL_GA), gate(COL_GB), gate(COL_GC),
            wup(A_WIDTH), wup(B_WIDTH), wup(C_WIDTH),
            pl.BlockSpec((tn, D_MODEL), lambda i, n: (n, 0)),
            pl.BlockSpec((tm, D_MODEL), lambda i, n: (i, 0)),
        ],
        out_specs=pl.BlockSpec((tm, D_MODEL), lambda i, n: (i, 0)),
        out_shape=jax.ShapeDtypeStruct((m, D_MODEL), F32),
        compiler_params=pltpu.CompilerParams(
            dimension_semantics=("parallel", "arbitrary"), vmem_limit_bytes=VMEM_LIMIT),
        name="merge_out",
    )(ya, yb, yc, proj2, proj2, proj2, wa, wb, wc, wo, x2d)


def _layer(x, mem, norm_g, w_in, a_q_g, a_k_g, a_rel_bias, w_up_a,
           b_mu_rkv, b_mu_w, b_mu_a, b_w0, b_w2, b_a0, b_a2, b_k_k, b_k_a, b_r_k,
           b_ln_w, b_ln_b, w_up_b, mem_norm_g, w_mem_kv, c_q_g, c_k_g, w_up_c, w_o):
    bsz, seq, d = x.shape
    t = bsz * seq
    x2d = x.reshape(t, d)
    lora_lo = 4 * A_WIDTH + 4 * B_WIDTH
    lora_hi = lora_lo + 2 * LORA
    w_main = jnp.concatenate([w_in[:, :lora_lo], w_in[:, lora_hi:]], axis=1).astype(BF16)
    w_lora = w_in[:, lora_lo:lora_hi].astype(BF16)

    proj = _norm_matmul(x2d, norm_g, w_main, 1024, 1024, "in_proj")
    lora = _norm_matmul(x2d, norm_g, w_lora, 1024, 2 * LORA, "in_proj_lora")
    mkv = _norm_matmul(mem.reshape(bsz * N_MEM, d), mem_norm_g, w_mem_kv.astype(BF16), 1024, 1024, "mem_kv")

    proj3 = proj.reshape(bsz, seq, MAIN_COLS)
    ya = _band_attn(proj3, _band_bias_table(a_rel_bias), a_q_g, a_k_g)
    yc = _mem_attn(proj3, mkv.reshape(bsz, N_MEM, 2 * C_WIDTH), c_q_g, c_k_g, 512)
    yb = _rwkv(proj3, lora.reshape(bsz, seq, 2 * LORA), b_mu_rkv,
               jnp.concatenate([b_mu_w, b_mu_a]), b_w0, b_a0,
               jnp.concatenate([b_w2, b_a2], axis=0), b_k_k, b_k_a, b_r_k.reshape(-1), b_ln_w, b_ln_b)

    out = _merge_out(ya.reshape(t, A_WIDTH), yb.reshape(t, B_WIDTH), yc.reshape(t, C_WIDTH), proj,
                     w_up_a.astype(BF16), w_up_b.astype(BF16), w_up_c.astype(BF16), w_o.astype(BF16),
                     x2d, 512, 512)
    return out.reshape(bsz, seq, d)


def kernel(x, mem, norm_g, w_in, a_q_g, a_k_g, a_rel_bias, w_up_a, b_mu_rkv, b_mu_w, b_mu_a, b_w0, b_w2, b_a0, b_a2, b_k_k, b_k_a, b_r_k, b_ln_w, b_ln_b, w_up_b, mem_norm_g, w_mem_kv, c_q_g, c_k_g, w_up_c, w_o):
    for l in range(norm_g.shape[0]):
        x = _layer(x, mem, norm_g[l], w_in[l], a_q_g[l], a_k_g[l], a_rel_bias[l], w_up_a[l],
                   b_mu_rkv[l], b_mu_w[l], b_mu_a[l], b_w0[l], b_w2[l], b_a0[l], b_a2[l],
                   b_k_k[l], b_k_a[l], b_r_k[l], b_ln_w[l], b_ln_b[l], w_up_b[l],
                   mem_norm_g[l], w_mem_kv[l], c_q_g[l], c_k_g[l], w_up_c[l], w_o[l])
    return x
```

```python
import functools

import numpy as np
import jax
import jax.numpy as jnp
from jax import lax
from jax.experimental import pallas as pl
from jax.experimental.pallas import tpu as pltpu

D_MODEL = 2048
CHUNK = 64
N_MEM = 256
NORM_EPS = 1e-6
NEG_INF = -1e30

A_HEADS = 8
A_HEAD_DIM = 128
A_WIDTH = A_HEADS * A_HEAD_DIM
A_LEFT_CHUNKS = 8
REL_CLIP = 128

B_HEADS = 16
B_HEAD_DIM = 64
B_WIDTH = B_HEADS * B_HEAD_DIM
LORA = 64
GN_EPS = 64e-5

C_HEADS = 4
C_HEAD_DIM = 256
C_WIDTH = C_HEADS * C_HEAD_DIM

LANES = 128
VMEM_LIMIT = 56 * 1024 * 1024

Q_GROUP = 4 * CHUNK
K_WINDOW = A_LEFT_CHUNKS * CHUNK + Q_GROUP
K_PAD = A_LEFT_CHUNKS * CHUNK

COL_AQ, COL_AK, COL_AV, COL_AZ = 0, 1, 2, 3
COL_BR, COL_BK, COL_BV, COL_BZ = 4, 5, 6, 7
COL_CQ, COL_CZ = 8, 9
COL_GA, COL_GB, COL_GC = 10, 12, 14
MAIN_COLS = 16 * 1024

BF16 = jnp.bfloat16
F32 = jnp.float32


def _bdot(a, b):
    return jnp.dot(a.astype(BF16), b.astype(BF16), preferred_element_type=F32)


def _bdot_nt(a, b):
    return lax.dot_general(a.astype(BF16), b.astype(BF16), (((1,), (1,)), ((), ())),
                           preferred_element_type=F32)


def _bdot_tn(a, b):
    return lax.dot_general(a.astype(BF16), b.astype(BF16), (((0,), (0,)), ((), ())),
                           preferred_element_type=F32)


def _split2(x):
    hi = x.astype(BF16)
    lo = (x - hi.astype(F32)).astype(BF16)
    return hi, lo


def _split3(x):
    hi = x.astype(BF16)
    r1 = x - hi.astype(F32)
    mid = r1.astype(BF16)
    lo = (r1 - mid.astype(F32)).astype(BF16)
    return hi, mid, lo


def _silu(x):
    return x * jax.nn.sigmoid(x)


def _norm_matmul_kernel(x_ref, g_ref, w_ref, o_ref, h_ref):
    @pl.when(pl.program_id(1) == 0)
    def _():
        x = x_ref[...]
        ms = jnp.mean(x * x, axis=-1, keepdims=True)
        h_ref[...] = (x * lax.rsqrt(ms + NORM_EPS) * g_ref[...]).astype(BF16)

    o_ref[...] = jnp.dot(h_ref[...], w_ref[...], preferred_element_type=F32).astype(o_ref.dtype)


def _norm_matmul(x2d, g, w_bf16, tm, tn, name):
    m, k = x2d.shape
    n = w_bf16.shape[1]
    return pl.pallas_call(
        _norm_matmul_kernel,
        grid=(m // tm, n // tn),
        in_specs=[
            pl.BlockSpec((tm, k), lambda i, j: (i, 0)),
            pl.BlockSpec((1, k), lambda i, j: (0, 0)),
            pl.BlockSpec((k, tn), lambda i, j: (0, j)),
        ],
        out_specs=pl.BlockSpec((tm, tn), lambda i, j: (i, j)),
        out_shape=jax.ShapeDtypeStruct((m, n), F32),
        scratch_shapes=[pltpu.VMEM((tm, k), BF16)],
        compiler_params=pltpu.CompilerParams(
            dimension_semantics=("parallel", "arbitrary"), vmem_limit_bytes=VMEM_LIMIT),
        name=name,
    )(x2d, g.reshape(1, k), w_bf16)


def _in_proj_kernel(x_ref, g_ref, wlo_ref, whi_ref, wla_ref, o_ref, la_ref, h_ref, *, n_lo):
    j = pl.program_id(1)

    @pl.when(j == 0)
    def _():
        x = x_ref[...]
        ms = jnp.mean(x * x, axis=-1, keepdims=True)
        h = (x * lax.rsqrt(ms + NORM_EPS) * g_ref[...]).astype(BF16)
        h_ref[...] = h
        la_ref[...] = jnp.dot(h, wla_ref[...], preferred_element_type=F32)

    @pl.when(j < n_lo)
    def _():
        o_ref[...] = jnp.dot(h_ref[...], wlo_ref[...], preferred_element_type=F32)

    @pl.when(j >= n_lo)
    def _():
        o_ref[...] = jnp.dot(h_ref[...], whi_ref[...], preferred_element_type=F32)


def _in_proj(x2d, g, w_lo, w_hi, w_lora, tm, tn):
    m, k = x2d.shape
    n_lo = w_lo.shape[1] // tn
    n_hi = w_hi.shape[1] // tn
    return pl.pallas_call(
        functools.partial(_in_proj_kernel, n_lo=n_lo),
        grid=(m // tm, n_lo + n_hi),
        in_specs=[
            pl.BlockSpec((tm, k), lambda i, j: (i, 0)),
            pl.BlockSpec((1, k), lambda i, j: (0, 0)),
            pl.BlockSpec((k, tn), lambda i, j: (0, jnp.minimum(j, n_lo - 1))),
            pl.BlockSpec((k, tn), lambda i, j: (0, jnp.maximum(j - n_lo, 0))),
            pl.BlockSpec((k, 2 * LORA), lambda i, j: (0, 0)),
        ],
        out_specs=[
            pl.BlockSpec((tm, tn), lambda i, j: (i, j)),
            pl.BlockSpec((tm, 2 * LORA), lambda i, j: (i, 0)),
        ],
        out_shape=[
            jax.ShapeDtypeStruct((m, (n_lo + n_hi) * tn), F32),
            jax.ShapeDtypeStruct((m, 2 * LORA), F32),
        ],
        scratch_shapes=[pltpu.VMEM((tm, k), BF16)],
        compiler_params=pltpu.CompilerParams(
            dimension_semantics=("parallel", "arbitrary"), vmem_limit_bytes=VMEM_LIMIT),
        name="in_proj",
    )(x2d, g.reshape(1, k), w_lo, w_hi, w_lora)


def _band_attn_kernel(q_ref, k_ref, v_ref, z_ref, bias_ref, gq_ref, gk_ref, o_ref,
                      qn_ref, kp_ref, vp_ref):
    seq = q_ref.shape[0]
    scale = A_HEAD_DIM ** -0.5
    q = q_ref[...]
    qn = q * lax.rsqrt(jnp.mean(q * q, axis=-1, keepdims=True) + NORM_EPS) * gq_ref[...]
    qn_ref[...] = (qn * scale).astype(BF16)
    k = k_ref[...]
    kn = k * lax.rsqrt(jnp.mean(k * k, axis=-1, keepdims=True) + NORM_EPS) * gk_ref[...]
    kp_ref[0:K_PAD, :] = jnp.zeros((K_PAD, A_HEAD_DIM), BF16)
    vp_ref[0:K_PAD, :] = jnp.zeros((K_PAD, A_HEAD_DIM), BF16)
    kp_ref[K_PAD:, :] = kn.astype(BF16)
    vp_ref[K_PAD:, :] = v_ref[...].astype(BF16)
    bias = bias_ref[...]
    col = lax.broadcasted_iota(jnp.int32, (Q_GROUP, K_WINDOW), 1)
    for g in range(seq // Q_GROUP):
        q0 = g * Q_GROUP
        s = _bdot_nt(qn_ref[q0:q0 + Q_GROUP, :], kp_ref[q0:q0 + K_WINDOW, :]) + bias
        if q0 < K_PAD:
            s = jnp.where(col >= K_PAD - q0, s, NEG_INF)
        m = jnp.max(s, axis=-1, keepdims=True)
        p = jnp.exp(s - m)
        l = jnp.sum(p, axis=-1, keepdims=True)
        o = _bdot(p, vp_ref[q0:q0 + K_WINDOW, :]) / l
        o_ref[q0:q0 + Q_GROUP, :] = (o * _silu(z_ref[q0:q0 + Q_GROUP, :])).astype(o_ref.dtype)


def _band_bias_table(rel_bias):
    period = 1024
    q = np.arange(period)
    m_minus_r = np.where(q < K_WINDOW, q, q - period)
    dist = K_PAD - m_minus_r
    f = rel_bias[:, np.clip(dist, -REL_CLIP, REL_CLIP) + REL_CLIP].astype(F32)
    seq = jnp.tile(f, (1, Q_GROUP))[:, :Q_GROUP * (period - 1)]
    toeplitz = seq.reshape(-1, Q_GROUP, period - 1)[:, :, :K_WINDOW]
    r = np.arange(Q_GROUP)[:, None]
    m = np.arange(K_WINDOW)[None, :]
    lo = (r // CHUNK) * CHUNK
    band = (m >= lo) & (m < lo + (A_LEFT_CHUNKS + 1) * CHUNK)
    return jnp.where(jnp.asarray(band)[None], toeplitz, NEG_INF)


def _band_attn(proj3, bias_tab, gq, gk):
    bsz, seq, _ = proj3.shape
    hb = A_WIDTH // A_HEAD_DIM

    def col(group):
        return pl.BlockSpec((None, seq, A_HEAD_DIM), lambda b, h: (b, 0, group * hb + h))

    return pl.pallas_call(
        _band_attn_kernel,
        grid=(bsz, A_HEADS),
        in_specs=[
            col(COL_AQ), col(COL_AK), col(COL_AV), col(COL_AZ),
            pl.BlockSpec((None, Q_GROUP, K_WINDOW), lambda b, h: (h, 0, 0)),
            pl.BlockSpec((1, A_HEAD_DIM), lambda b, h: (0, 0)),
            pl.BlockSpec((1, A_HEAD_DIM), lambda b, h: (0, 0)),
        ],
        out_specs=pl.BlockSpec((None, seq, A_HEAD_DIM), lambda b, h: (b, 0, h)),
        out_shape=jax.ShapeDtypeStruct((bsz, seq, A_WIDTH), BF16),
        scratch_shapes=[
            pltpu.VMEM((seq, A_HEAD_DIM), BF16),
            pltpu.VMEM((seq + K_PAD, A_HEAD_DIM), BF16),
            pltpu.VMEM((seq + K_PAD, A_HEAD_DIM), BF16),
        ],
        compiler_params=pltpu.CompilerParams(
            dimension_semantics=("parallel", "parallel"), vmem_limit_bytes=VMEM_LIMIT),
        name="band_attn",
    )(proj3, proj3, proj3, proj3, bias_tab, gq.reshape(1, -1), gk.reshape(1, -1))


def _mem_attn_kernel(q_ref, z_ref, mk_ref, mv_ref, gq_ref, gk_ref, o_ref):
    scale = C_HEAD_DIM ** -0.5
    for h in range(C_HEADS):
        sl = slice(h * C_HEAD_DIM, (h + 1) * C_HEAD_DIM)
        q = q_ref[:, sl]
        qn = q * lax.rsqrt(jnp.mean(q * q, axis=-1, keepdims=True) + NORM_EPS) * gq_ref[...] * scale
        k = mk_ref[:, sl]
        kn = k * lax.rsqrt(jnp.mean(k * k, axis=-1, keepdims=True) + NORM_EPS) * gk_ref[...]
        s = _bdot_nt(qn, kn)
        m = jnp.max(s, axis=-1, keepdims=True)
        p = jnp.exp(s - m)
        l = jnp.sum(p, axis=-1, keepdims=True)
        o = _bdot(p, mv_ref[:, sl]) / l
        o_ref[:, sl] = (o * _silu(z_ref[:, sl])).astype(o_ref.dtype)


def _mem_attn(proj3, mkv3, gq, gk, ts):
    bsz, seq, _ = proj3.shape
    return pl.pallas_call(
        _mem_attn_kernel,
        grid=(bsz, seq // ts),
        in_specs=[
            pl.BlockSpec((None, ts, C_WIDTH), lambda b, s: (b, s, COL_CQ)),
            pl.BlockSpec((None, ts, C_WIDTH), lambda b, s: (b, s, COL_CZ)),
            pl.BlockSpec((None, N_MEM, C_WIDTH), lambda b, s: (b, 0, 0)),
            pl.BlockSpec((None, N_MEM, C_WIDTH), lambda b, s: (b, 0, 1)),
            pl.BlockSpec((1, C_HEAD_DIM), lambda b, s: (0, 0)),
            pl.BlockSpec((1, C_HEAD_DIM), lambda b, s: (0, 0)),
        ],
        out_specs=pl.BlockSpec((None, ts, C_WIDTH), lambda b, s: (b, s, 0)),
        out_shape=jax.ShapeDtypeStruct((bsz, seq, C_WIDTH), BF16),
        compiler_params=pltpu.CompilerParams(
            dimension_semantics=("parallel", "parallel"), vmem_limit_bytes=VMEM_LIMIT),
        name="mem_attn",
    )(proj3, proj3, mkv3, mkv3, gq.reshape(1, -1), gk.reshape(1, -1))


def _shift_rows(x, carry_row):
    rolled = pltpu.roll(x, 1, 0)
    row = lax.broadcasted_iota(jnp.int32, x.shape, 0)
    return jnp.where(row == 0, carry_row, rolled)


def _seg_sum(x, seg_ones):
    hi, lo = _split2(x)
    return (jnp.dot(hi, seg_ones, preferred_element_type=F32)
            + jnp.dot(lo, seg_ones, preferred_element_type=F32))


def _rwkv_kernel(pr_ref, pk_ref, pv_ref, pz_ref, la_ref,
                 mur_ref, muk_ref, muv_ref, mula_ref, w0_ref, a0_ref, w2a2_ref,
                 kk_ref, ka_ref, rk_ref, lnw_ref, lnb_ref,
                 o_ref,
                 st_ref, cr_ref, ck_ref, cv_ref, cla_ref):
    TB = pr_ref.shape[0]
    L = CHUNK
    NC = TB // L

    @pl.when(pl.program_id(1) == 0)
    def _():
        st_ref[...] = jnp.zeros_like(st_ref)
        cr_ref[...] = jnp.zeros_like(cr_ref)
        ck_ref[...] = jnp.zeros_like(ck_ref)
        cv_ref[...] = jnp.zeros_like(cv_ref)
        cla_ref[...] = jnp.zeros_like(cla_ref)

    def lerp(x_ref, carry_ref, mu_ref):
        x = x_ref[...]
        prev = _shift_rows(x, carry_ref[0:1, :])
        carry_ref[0:1, :] = x[TB - 1:TB, :]
        return x + mu_ref[...] * (prev - x)

    r = lerp(pr_ref, cr_ref, mur_ref)
    k = lerp(pk_ref, ck_ref, muk_ref)
    v = lerp(pv_ref, cv_ref, muv_ref)
    la = lerp(la_ref, cla_ref, mula_ref)

    is_wd = lax.broadcasted_iota(jnp.int32, (TB, LANES), 1) < LORA
    xw = jnp.where(is_wd, jnp.tanh(la), 0.0)
    xa = jnp.where(is_wd, 0.0, la)
    lo_half = lax.broadcasted_iota(jnp.int32, (L, LANES), 1) < B_HEAD_DIM
    w_hi, w_lo = _split2(w2a2_ref[...])

    def dot3(x):
        x_hi, x_lo = _split2(x)
        return (jnp.dot(x_hi, w_hi, preferred_element_type=F32)
                + jnp.dot(x_lo, w_hi, preferred_element_type=F32)
                + jnp.dot(x_hi, w_lo, preferred_element_type=F32))

    zw = -(w0_ref[...] + dot3(xw))
    softplus = jnp.maximum(zw, 0.0) + jnp.log(1.0 + jnp.exp(-jnp.abs(zw)))
    lw = -jnp.exp(-softplus - 0.5)
    a_sig = jax.nn.sigmoid(a0_ref[...] + dot3(xa))

    ti = lax.broadcasted_iota(jnp.int32, (TB, TB), 0)
    si = lax.broadcasted_iota(jnp.int32, (TB, TB), 1)
    tri = ((si <= ti) & ((si & -L) == (ti & -L))).astype(BF16)
    l_hi, l_mid, l_lo = _split3(lw)
    cum = (jnp.dot(tri, l_hi, preferred_element_type=F32)
           + jnp.dot(tri, l_mid, preferred_element_type=F32)
           + jnp.dot(tri, l_lo, preferred_element_type=F32))
    e_pos = jnp.exp(cum)
    e_neg = jnp.exp(-cum)
    e_prev = jnp.exp(cum - lw)

    ri = lax.broadcasted_iota(jnp.int32, (LANES, LANES), 0)
    ci = lax.broadcasted_iota(jnp.int32, (LANES, LANES), 1)
    seg_ones = ((ri < B_HEAD_DIM) == (ci < B_HEAD_DIM)).astype(BF16)
    inv_n = 1.0 / B_HEAD_DIM
    same_head = (ri < B_HEAD_DIM) == (ci < B_HEAD_DIM)
    mi = lax.broadcasted_iota(jnp.int32, (2 * L, 4 * L), 0)
    mj = lax.broadcasted_iota(jnp.int32, (2 * L, 4 * L), 1)
    tt, ss = mi & (L - 1), mj & (L - 1)
    tri_mask = (ss < tt) | ((mi >= L) & (ss == tt))
    lane2 = lax.broadcasted_iota(jnp.int32, (L, 2 * LANES), 1)
    lo_half2 = (lane2 & B_HEAD_DIM) == 0

    def stack_heads(x2):
        lo = lo_half if x2.shape[1] == LANES else lo_half2
        return jnp.concatenate([jnp.where(lo, x2, 0.0), jnp.where(lo, 0.0, x2)], axis=0).astype(BF16)

    n_pairs = B_HEADS // 2
    units = [(slice(c * L, (c + 1) * L), slice(p * LANES, (p + 1) * LANES))
             for c in range(NC) for p in range(n_pairs)]
    nu = range(len(units))

    kk_raw = [k[u] * kk_ref[:, u[1]] for u in units]
    kmod = [k[u] * (1.0 + (a_sig[u] - 1.0) * ka_ref[:, u[1]]) for u in units]
    sums = [_seg_sum(jnp.concatenate([kk_raw[i] * kk_raw[i], r[units[i]] * kmod[i] * rk_ref[:, units[i][1]]],
                                     axis=0), seg_ones) for i in nu]
    kk = [kk_raw[i] * lax.rsqrt(jnp.maximum(sums[i][:L], 1e-24)) for i in nu]
    bonus = [sums[i][L:] * v[units[i]] for i in nu]
    vb = [v[u].astype(BF16) for u in units]
    vs = [stack_heads(v[u]) for u in units]
    rt = [r[u] * e_pos[u] for u in units]
    at = [(-kk[i]) * e_prev[units[i]] for i in nu]
    bt = [(kk[i] * a_sig[units[i]]) * e_neg[units[i]] for i in nu]
    kt = [kmod[i] * e_neg[units[i]] for i in nu]
    bks = [jnp.concatenate([stack_heads(bt[i]), stack_heads(kt[i])], axis=0) for i in nu]

    sm = [jnp.where(tri_mask, _bdot_nt(jnp.concatenate([at[i], rt[i]], axis=0), bks[i]), 0.0).astype(BF16)
          for i in nu]
    av = [jnp.dot(sm[i][:, 2 * L:], vs[i], preferred_element_type=F32) for i in nu]

    xs = [jnp.concatenate([at[i], av[i][:L]], axis=1) for i in nu]
    pw = [sm[i][:L, :2 * L] for i in nu]
    n = 1
    while n < L:
        xs = [xs[i] + jnp.dot(pw[i], stack_heads(xs[i]), preferred_element_type=F32) for i in nu]
        n *= 2
        if n < L:
            pw = [jnp.dot(pw[i], stack_heads(pw[i]), preferred_element_type=F32).astype(BF16) for i in nu]

    st = [st_ref[p] for p in range(n_pairs)]
    o2 = []
    for c in range(NC):
        ids = range(c * n_pairs, (c + 1) * n_pairs)
        hs = [_bdot_nt(jnp.concatenate([xs[i][:, :LANES], rt[i]], axis=0), st[i - c * n_pairs]) for i in ids]
        us = [hs[j][:L] + xs[i][:, LANES:] for j, i in enumerate(ids)]
        o2 += [hs[j][L:] + jnp.dot(sm[i][L:, :2 * L], stack_heads(us[j]), preferred_element_type=F32)
               + av[i][L:] for j, i in enumerate(ids)]
        for j, i in enumerate(ids):
            upd = _bdot_tn(jnp.concatenate([us[j].astype(BF16), vb[i]], axis=0),
                           jnp.concatenate([bt[i], kt[i]], axis=0))
            last = units[i][0].stop - 1
            st[j] = (st[j] + jnp.where(same_head, upd, 0.0)) * e_pos[last:last + 1, units[i][1]]
    for p in range(n_pairs):
        st_ref[p] = st[p]

    mean = [_seg_sum(o2[i], seg_ones) * inv_n for i in nu]
    dev = [o2[i] - mean[i] for i in nu]
    var = [_seg_sum(dev[i] * dev[i], seg_ones) * inv_n for i in nu]
    for i in nu:
        rows, sl = units[i]
        on = dev[i] * lax.rsqrt(var[i] + GN_EPS) * lnw_ref[:, sl] + lnb_ref[:, sl]
        o_ref[rows, sl] = ((on + bonus[i]) * _silu(pz_ref[rows, sl])).astype(o_ref.dtype)


def _rwkv(proj3, lora3, mu_rkv, mu_la, w0, a0, w2a2, k_k, k_a, r_k, ln_w, ln_b, tokens_per_step):
    bsz, seq, _ = proj3.shape
    L = tokens_per_step

    def col(group):
        return pl.BlockSpec((None, L, B_WIDTH), lambda b, c: (b, c, group))

    def vec(n):
        return pl.BlockSpec((1, n), lambda b, c: (0, 0))

    row = lambda t: t.reshape(1, -1)
    return pl.pallas_call(
        _rwkv_kernel,
        grid=(bsz, seq // L),
        in_specs=[
            col(COL_BR), col(COL_BK), col(COL_BV), col(COL_BZ),
            pl.BlockSpec((None, L, 2 * LORA), lambda b, c: (b, c, 0)),
            vec(B_WIDTH), vec(B_WIDTH), vec(B_WIDTH), vec(2 * LORA), vec(B_WIDTH), vec(B_WIDTH),
            pl.BlockSpec((2 * LORA, B_WIDTH), lambda b, c: (0, 0)),
            vec(B_WIDTH), vec(B_WIDTH), vec(B_WIDTH), vec(B_WIDTH), vec(B_WIDTH),
        ],
        out_specs=pl.BlockSpec((None, L, B_WIDTH), lambda b, c: (b, c, 0)),
        out_shape=jax.ShapeDtypeStruct((bsz, seq, B_WIDTH), BF16),
        scratch_shapes=[
            pltpu.VMEM((B_HEADS // 2, LANES, LANES), F32),
            pltpu.VMEM((8, B_WIDTH), F32), pltpu.VMEM((8, B_WIDTH), F32), pltpu.VMEM((8, B_WIDTH), F32),
            pltpu.VMEM((8, 2 * LORA), F32),
        ],
        compiler_params=pltpu.CompilerParams(
            dimension_semantics=("parallel", "arbitrary"), vmem_limit_bytes=VMEM_LIMIT),
        name="rwkv",
    )(proj3, proj3, proj3, proj3, lora3,
      row(mu_rkv[0]), row(mu_rkv[1]), row(mu_rkv[2]), row(mu_la), row(w0), row(a0), w2a2,
      row(k_k), row(k_a), row(r_k), row(ln_w), row(ln_b))


def _merge_out_kernel(ya_ref, yb_ref, yc_ref, ga_ref, gb_ref, gc_ref, wa_ref, wb_ref, wc_ref, wo_ref, x_ref,
                      o_ref):
    merged = (jax.nn.sigmoid(ga_ref[...]) * jnp.dot(ya_ref[...], wa_ref[...], preferred_element_type=F32)
              + jax.nn.sigmoid(gb_ref[...]) * jnp.dot(yb_ref[...], wb_ref[...], preferred_element_type=F32)
              + jax.nn.sigmoid(gc_ref[...]) * jnp.dot(yc_ref[...], wc_ref[...], preferred_element_type=F32))
    o_ref[...] = x_ref[...] + jnp.dot(merged.astype(BF16), wo_ref[...], preferred_element_type=F32)


def _merge_out(ya, yb, yc, proj2, wa, wb, wc, wo, x2d, tm):
    m = x2d.shape[0]

    def ybranch(width):
        return pl.BlockSpec((tm, width), lambda i: (i, 0))

    def gate(group):
        return pl.BlockSpec((tm, D_MODEL), lambda i: (i, group * 1024 // D_MODEL))

    def resident(shape):
        return pl.BlockSpec(shape, lambda i: (0, 0), pipeline_mode=pl.Buffered(1))

    return pl.pallas_call(
        _merge_out_kernel,
        grid=(m // tm,),
        in_specs=[
            ybranch(A_WIDTH), ybranch(B_WIDTH), ybranch(C_WIDTH),
            gate(COL_GA), gate(COL_GB), gate(COL_GC),
            resident((A_WIDTH, D_MODEL)), resident((B_WIDTH, D_MODEL)), resident((C_WIDTH, D_MODEL)),
            resident((D_MODEL, D_MODEL)),
            pl.BlockSpec((tm, D_MODEL), lambda i: (i, 0)),
        ],
        out_specs=pl.BlockSpec((tm, D_MODEL), lambda i: (i, 0)),
        out_shape=jax.ShapeDtypeStruct((m, D_MODEL), F32),
        compiler_params=pltpu.CompilerParams(
            dimension_semantics=("parallel",), vmem_limit_bytes=VMEM_LIMIT),
        name="merge_out",
    )(ya, yb, yc, proj2, proj2, proj2, wa, wb, wc, wo, x2d)


def _layer(x, mem, norm_g, w_in, a_q_g, a_k_g, a_rel_bias, w_up_a,
           b_mu_rkv, b_mu_w, b_mu_a, b_w0, b_w2, b_a0, b_a2, b_k_k, b_k_a, b_r_k,
           b_ln_w, b_ln_b, w_up_b, mem_norm_g, w_mem_kv, c_q_g, c_k_g, w_up_c, w_o):
    bsz, seq, d = x.shape
    t = bsz * seq
    x2d = x.reshape(t, d)
    lora_lo = 4 * A_WIDTH + 4 * B_WIDTH
    lora_hi = lora_lo + 2 * LORA
    w_lo = w_in[:, :lora_lo].astype(BF16)
    w_hi = w_in[:, lora_hi:].astype(BF16)
    w_lora = w_in[:, lora_lo:lora_hi].astype(BF16)

    proj, lora = _in_proj(x2d, norm_g, w_lo, w_hi, w_lora, 1024, 1024)
    mkv = _norm_matmul(mem.reshape(bsz * N_MEM, d), mem_norm_g, w_mem_kv.astype(BF16), 1024, 1024, "mem_kv")

    proj3 = proj.reshape(bsz, seq, MAIN_COLS)
    ya = _band_attn(proj3, _band_bias_table(a_rel_bias), a_q_g, a_k_g)
    yc = _mem_attn(proj3, mkv.reshape(bsz, N_MEM, 2 * C_WIDTH), c_q_g, c_k_g, 512)
    yb = _rwkv(proj3, lora.reshape(bsz, seq, 2 * LORA), b_mu_rkv,
               jnp.concatenate([b_mu_w, b_mu_a]), b_w0, b_a0,
               jnp.concatenate([b_w2, b_a2], axis=0), b_k_k, b_k_a, b_r_k.reshape(-1), b_ln_w, b_ln_b, 2 * CHUNK)

    out = _merge_out(ya.reshape(t, A_WIDTH), yb.reshape(t, B_WIDTH), yc.reshape(t, C_WIDTH), proj,
                     w_up_a.astype(BF16), w_up_b.astype(BF16), w_up_c.astype(BF16), w_o.astype(BF16),
                     x2d, 256)
    return out.reshape(bsz, seq, d)


def kernel(x, mem, norm_g, w_in, a_q_g, a_k_g, a_rel_bias, w_up_a, b_mu_rkv, b_mu_w, b_mu_a, b_w0, b_w2, b_a0, b_a2, b_k_k, b_k_a, b_r_k, b_ln_w, b_ln_b, w_up_b, mem_norm_g, w_mem_kv, c_q_g, c_k_g, w_up_c, w_o):
    for l in range(norm_g.shape[0]):
        x = _layer(x, mem, norm_g[l], w_in[l], a_q_g[l], a_k_g[l], a_rel_bias[l], w_up_a[l],
                   b_mu_rkv[l], b_mu_w[l], b_mu_a[l], b_w0[l], b_w2[l], b_a0[l], b_a2[l],
                   b_k_k[l], b_k_a[l], b_r_k[l], b_ln_w[l], b_ln_b[l], w_up_b[l],
                   mem_norm_g[l], w_mem_kv[l], c_q_g[l], c_k_g[l], w_up_c[l], w_o[l])
    return x
```

```python
import numpy as np
import jax
import jax.numpy as jnp
from jax import lax
from jax.experimental import pallas as pl
from jax.experimental.pallas import tpu as pltpu

D_MODEL = 2048
CHUNK = 64
N_MEM = 256
NORM_EPS = 1e-6
NEG_INF = -1e30

A_HEADS = 8
A_HEAD_DIM = 128
A_WIDTH = A_HEADS * A_HEAD_DIM
A_LEFT_CHUNKS = 8
REL_CLIP = 128

B_HEADS = 16
B_HEAD_DIM = 64
B_WIDTH = B_HEADS * B_HEAD_DIM
LORA = 64
GN_EPS = 64e-5

C_HEADS = 4
C_HEAD_DIM = 256
C_WIDTH = C_HEADS * C_HEAD_DIM

LANES = 128
VMEM_LIMIT = 56 * 1024 * 1024

Q_GROUP = 4 * CHUNK
K_WINDOW = A_LEFT_CHUNKS * CHUNK + Q_GROUP
K_PAD = A_LEFT_CHUNKS * CHUNK

COL_AQ, COL_AK, COL_AV, COL_AZ = 0, 1, 2, 3
COL_BR, COL_BK, COL_BV, COL_BZ = 4, 5, 6, 7
COL_CQ, COL_CZ = 8, 9
COL_GA, COL_GB, COL_GC = 10, 12, 14
MAIN_COLS = 16 * 1024
LORA_LO = 4 * A_WIDTH + 4 * B_WIDTH
LORA_HI = LORA_LO + 2 * LORA

BF16 = jnp.bfloat16
F32 = jnp.float32


def _bdot(a, b):
    return jnp.dot(a.astype(BF16), b.astype(BF16), preferred_element_type=F32)


def _bdot_nt(a, b):
    return lax.dot_general(a.astype(BF16), b.astype(BF16), (((1,), (1,)), ((), ())),
                           preferred_element_type=F32)


def _bdot_tn(a, b):
    return lax.dot_general(a.astype(BF16), b.astype(BF16), (((0,), (0,)), ((), ())),
                           preferred_element_type=F32)


def _split2(x):
    hi = x.astype(BF16)
    lo = (x - hi.astype(F32)).astype(BF16)
    return hi, lo


def _split3(x):
    hi = x.astype(BF16)
    r1 = x - hi.astype(F32)
    mid = r1.astype(BF16)
    lo = (r1 - mid.astype(F32)).astype(BF16)
    return hi, mid, lo


def _silu(x):
    return x * jax.nn.sigmoid(x)


def _prep_w_in_kernel(w_ref, main_ref, lora_ref):
    main_ref[:, :LORA_LO] = w_ref[:, :LORA_LO].astype(BF16)
    main_ref[:, LORA_LO:] = w_ref[:, LORA_HI:].astype(BF16)
    lora_ref[...] = w_ref[:, LORA_LO:LORA_HI].astype(BF16)


def _prep_w_in(w_in, rows):
    k, n = w_in.shape
    return pl.pallas_call(
        _prep_w_in_kernel,
        grid=(k // rows,),
        in_specs=[pl.BlockSpec((rows, n), lambda i: (i, 0))],
        out_specs=[
            pl.BlockSpec((rows, MAIN_COLS), lambda i: (i, 0)),
            pl.BlockSpec((rows, 2 * LORA), lambda i: (i, 0)),
        ],
        out_shape=[
            jax.ShapeDtypeStruct((k, MAIN_COLS), BF16),
            jax.ShapeDtypeStruct((k, 2 * LORA), BF16),
        ],
        compiler_params=pltpu.CompilerParams(
            dimension_semantics=("parallel",), vmem_limit_bytes=VMEM_LIMIT),
        name="prep_w_in",
    )(w_in)


def _norm_matmul_kernel(x_ref, g_ref, w_ref, o_ref, h_ref):
    @pl.when(pl.program_id(1) == 0)
    def _():
        x = x_ref[...]
        ms = jnp.mean(x * x, axis=-1, keepdims=True)
        h_ref[...] = (x * lax.rsqrt(ms + NORM_EPS) * g_ref[...]).astype(BF16)

    o_ref[...] = jnp.dot(h_ref[...], w_ref[...], preferred_element_type=F32).astype(o_ref.dtype)


def _norm_matmul(x2d, g, w_bf16, tm, tn, name):
    m, k = x2d.shape
    n = w_bf16.shape[1]
    return pl.pallas_call(
        _norm_matmul_kernel,
        grid=(m // tm, n // tn),
        in_specs=[
            pl.BlockSpec((tm, k), lambda i, j: (i, 0)),
            pl.BlockSpec((1, k), lambda i, j: (0, 0)),
            pl.BlockSpec((k, tn), lambda i, j: (0, j)),
        ],
        out_specs=pl.BlockSpec((tm, tn), lambda i, j: (i, j)),
        out_shape=jax.ShapeDtypeStruct((m, n), F32),
        scratch_shapes=[pltpu.VMEM((tm, k), BF16)],
        compiler_params=pltpu.CompilerParams(
            dimension_semantics=("parallel", "arbitrary"), vmem_limit_bytes=VMEM_LIMIT),
        name=name,
    )(x2d, g.reshape(1, k), w_bf16)


def _in_proj_kernel(x_ref, g_ref, w_ref, wla_ref, o_ref, la_ref, h_ref):
    @pl.when(pl.program_id(1) == 0)
    def _():
        x = x_ref[...]
        ms = jnp.mean(x * x, axis=-1, keepdims=True)
        h = (x * lax.rsqrt(ms + NORM_EPS) * g_ref[...]).astype(BF16)
        h_ref[...] = h
        la_ref[...] = jnp.dot(h, wla_ref[...], preferred_element_type=F32)

    o_ref[...] = jnp.dot(h_ref[...], w_ref[...], preferred_element_type=F32)


def _in_proj(x2d, g, w_main, w_lora, tm, tn):
    m, k = x2d.shape
    n = w_main.shape[1]
    return pl.pallas_call(
        _in_proj_kernel,
        grid=(m // tm, n // tn),
        in_specs=[
            pl.BlockSpec((tm, k), lambda i, j: (i, 0)),
            pl.BlockSpec((1, k), lambda i, j: (0, 0)),
            pl.BlockSpec((k, tn), lambda i, j: (0, j)),
            pl.BlockSpec((k, 2 * LORA), lambda i, j: (0, 0)),
        ],
        out_specs=[
            pl.BlockSpec((tm, tn), lambda i, j: (i, j)),
            pl.BlockSpec((tm, 2 * LORA), lambda i, j: (i, 0)),
        ],
        out_shape=[
            jax.ShapeDtypeStruct((m, n), F32),
            jax.ShapeDtypeStruct((m, 2 * LORA), F32),
        ],
        scratch_shapes=[pltpu.VMEM((tm, k), BF16)],
        compiler_params=pltpu.CompilerParams(
            dimension_semantics=("parallel", "arbitrary"), vmem_limit_bytes=VMEM_LIMIT),
        name="in_proj",
    )(x2d, g.reshape(1, k), w_main, w_lora)


def _band_attn_kernel(q_ref, k_ref, v_ref, z_ref, bias_ref, gq_ref, gk_ref, o_ref,
                      qn_ref, kp_ref, vp_ref):
    seq = q_ref.shape[0]
    scale = A_HEAD_DIM ** -0.5
    q = q_ref[...]
    qn = q * lax.rsqrt(jnp.mean(q * q, axis=-1, keepdims=True) + NORM_EPS) * gq_ref[...]
    qn_ref[...] = (qn * scale).astype(BF16)
    k = k_ref[...]
    kn = k * lax.rsqrt(jnp.mean(k * k, axis=-1, keepdims=True) + NORM_EPS) * gk_ref[...]
    kp_ref[0:K_PAD, :] = jnp.zeros((K_PAD, A_HEAD_DIM), BF16)
    vp_ref[0:K_PAD, :] = jnp.zeros((K_PAD, A_HEAD_DIM), BF16)
    kp_ref[K_PAD:, :] = kn.astype(BF16)
    vp_ref[K_PAD:, :] = v_ref[...].astype(BF16)
    for g in range(seq // Q_GROUP):
        q0 = g * Q_GROUP
        pad = max(K_PAD - q0, 0)
        s = _bdot_nt(qn_ref[q0:q0 + Q_GROUP, :], kp_ref[q0 + pad:q0 + K_WINDOW, :]) + bias_ref[:, pad:]
        m = jnp.max(s, axis=-1, keepdims=True)
        p = jnp.exp(s - m)
        l = jnp.sum(p, axis=-1, keepdims=True)
        o = _bdot(p, vp_ref[q0 + pad:q0 + K_WINDOW, :]) / l
        o_ref[q0:q0 + Q_GROUP, :] = (o * _silu(z_ref[q0:q0 + Q_GROUP, :])).astype(o_ref.dtype)


def _band_bias_table(rel_bias):
    period = 1024
    q = np.arange(period)
    m_minus_r = np.where(q < K_WINDOW, q, q - period)
    dist = K_PAD - m_minus_r
    f = rel_bias[:, np.clip(dist, -REL_CLIP, REL_CLIP) + REL_CLIP].astype(F32)
    seq = jnp.tile(f, (1, Q_GROUP))[:, :Q_GROUP * (period - 1)]
    toeplitz = seq.reshape(-1, Q_GROUP, period - 1)[:, :, :K_WINDOW]
    r = np.arange(Q_GROUP)[:, None]
    m = np.arange(K_WINDOW)[None, :]
    lo = (r // CHUNK) * CHUNK
    band = (m >= lo) & (m < lo + (A_LEFT_CHUNKS + 1) * CHUNK)
    return jnp.where(jnp.asarray(band)[None], toeplitz, NEG_INF)


def _band_attn(proj3, bias_tab, gq, gk):
    bsz, seq, _ = proj3.shape
    hb = A_WIDTH // A_HEAD_DIM

    def col(group):
        return pl.BlockSpec((None, seq, A_HEAD_DIM), lambda b, h: (b, 0, group * hb + h))

    return pl.pallas_call(
        _band_attn_kernel,
        grid=(bsz, A_HEADS),
        in_specs=[
            col(COL_AQ), col(COL_AK), col(COL_AV), col(COL_AZ),
            pl.BlockSpec((None, Q_GROUP, K_WINDOW), lambda b, h: (h, 0, 0)),
            pl.BlockSpec((1, A_HEAD_DIM), lambda b, h: (0, 0)),
            pl.BlockSpec((1, A_HEAD_DIM), lambda b, h: (0, 0)),
        ],
        out_specs=pl.BlockSpec((None, seq, A_HEAD_DIM), lambda b, h: (b, 0, h)),
        out_shape=jax.ShapeDtypeStruct((bsz, seq, A_WIDTH), BF16),
        scratch_shapes=[
            pltpu.VMEM((seq, A_HEAD_DIM), BF16),
            pltpu.VMEM((seq + K_PAD, A_HEAD_DIM), BF16),
            pltpu.VMEM((seq + K_PAD, A_HEAD_DIM), BF16),
        ],
        compiler_params=pltpu.CompilerParams(
            dimension_semantics=("parallel", "parallel"), vmem_limit_bytes=VMEM_LIMIT),
        name="band_attn",
    )(proj3, proj3, proj3, proj3, bias_tab, gq.reshape(1, -1), gk.reshape(1, -1))


def _mem_attn_kernel(q_ref, z_ref, mk_ref, mv_ref, gq_ref, gk_ref, o_ref):
    scale = C_HEAD_DIM ** -0.5
    for h in range(C_HEADS):
        sl = slice(h * C_HEAD_DIM, (h + 1) * C_HEAD_DIM)
        q = q_ref[:, sl]
        qn = q * lax.rsqrt(jnp.mean(q * q, axis=-1, keepdims=True) + NORM_EPS) * gq_ref[...] * scale
        k = mk_ref[:, sl]
        kn = k * lax.rsqrt(jnp.mean(k * k, axis=-1, keepdims=True) + NORM_EPS) * gk_ref[...]
        s = _bdot_nt(qn, kn)
        m = jnp.max(s, axis=-1, keepdims=True)
        p = jnp.exp(s - m)
        l = jnp.sum(p, axis=-1, keepdims=True)
        o = _bdot(p, mv_ref[:, sl]) / l
        o_ref[:, sl] = (o * _silu(z_ref[:, sl])).astype(o_ref.dtype)


def _mem_attn(proj3, mkv3, gq, gk, ts):
    bsz, seq, _ = proj3.shape
    return pl.pallas_call(
        _mem_attn_kernel,
        grid=(bsz, seq // ts),
        in_specs=[
            pl.BlockSpec((None, ts, C_WIDTH), lambda b, s: (b, s, COL_CQ)),
            pl.BlockSpec((None, ts, C_WIDTH), lambda b, s: (b, s, COL_CZ)),
            pl.BlockSpec((None, N_MEM, C_WIDTH), lambda b, s: (b, 0, 0)),
            pl.BlockSpec((None, N_MEM, C_WIDTH), lambda b, s: (b, 0, 1)),
            pl.BlockSpec((1, C_HEAD_DIM), lambda b, s: (0, 0)),
            pl.BlockSpec((1, C_HEAD_DIM), lambda b, s: (0, 0)),
        ],
        out_specs=pl.BlockSpec((None, ts, C_WIDTH), lambda b, s: (b, s, 0)),
        out_shape=jax.ShapeDtypeStruct((bsz, seq, C_WIDTH), BF16),
        compiler_params=pltpu.CompilerParams(
            dimension_semantics=("parallel", "parallel"), vmem_limit_bytes=VMEM_LIMIT),
        name="mem_attn",
    )(proj3, proj3, mkv3, mkv3, gq.reshape(1, -1), gk.reshape(1, -1))


def _shift_rows(x, carry_row):
    rolled = pltpu.roll(x, 1, 0)
    row = lax.broadcasted_iota(jnp.int32, x.shape, 0)
    return jnp.where(row == 0, carry_row, rolled)


def _seg_sum(x, seg_ones):
    hi, lo = _split2(x)
    return (jnp.dot(hi, seg_ones, preferred_element_type=F32)
            + jnp.dot(lo, seg_ones, preferred_element_type=F32))


def _drain(gen):
    for _ in gen:
        pass


def _rwkv_kernel(pr_ref, pk_ref, pv_ref, pz_ref, la_ref,
                 mur_ref, muk_ref, muv_ref, mula_ref, w0_ref, a0_ref, w2a2_ref,
                 kk_ref, ka_ref, rk_ref, lnw_ref, lnb_ref,
                 o_ref,
                 st_ref, cr_ref, ck_ref, cv_ref, cla_ref):
    L = CHUNK
    n_chunks = pr_ref.shape[0] // L
    n_pairs = B_HEADS // 2
    pair_lanes = [slice(p * LANES, (p + 1) * LANES) for p in range(n_pairs)]

    @pl.when(pl.program_id(1) == 0)
    def _():
        st_ref[...] = jnp.zeros_like(st_ref)
        cr_ref[...] = jnp.zeros_like(cr_ref)
        ck_ref[...] = jnp.zeros_like(ck_ref)
        cv_ref[...] = jnp.zeros_like(cv_ref)
        cla_ref[...] = jnp.zeros_like(cla_ref)

    lane = lax.broadcasted_iota(jnp.int32, (L, LANES), 1)
    lo_half = lane < B_HEAD_DIM
    lo_half2 = (lax.broadcasted_iota(jnp.int32, (L, 2 * LANES), 1) & B_HEAD_DIM) == 0
    ti = lax.broadcasted_iota(jnp.int32, (L, L), 0)
    si = lax.broadcasted_iota(jnp.int32, (L, L), 1)
    tri = (si <= ti).astype(BF16)
    ri = lax.broadcasted_iota(jnp.int32, (LANES, LANES), 0)
    ci = lax.broadcasted_iota(jnp.int32, (LANES, LANES), 1)
    same_head = (ri < B_HEAD_DIM) == (ci < B_HEAD_DIM)
    seg_ones = same_head.astype(BF16)
    inv_n = 1.0 / B_HEAD_DIM
    mi = lax.broadcasted_iota(jnp.int32, (2 * L, 4 * L), 0)
    mj = lax.broadcasted_iota(jnp.int32, (2 * L, 4 * L), 1)
    tt, ss = mi & (L - 1), mj & (L - 1)
    tri_mask = (ss < tt) | ((mi >= L) & (ss == tt))
    w_hi, w_lo = _split2(w2a2_ref[...])

    def stack_heads(x2):
        lo = lo_half if x2.shape[1] == LANES else lo_half2
        xb = x2.astype(BF16)
        return jnp.concatenate([jnp.where(lo, xb, 0.0), jnp.where(lo, 0.0, xb)], axis=0).astype(BF16)

    def dot3(x):
        x_hi, x_lo = _split2(x)
        return (jnp.dot(x_hi, w_hi, preferred_element_type=F32)
                + jnp.dot(x_lo, w_hi, preferred_element_type=F32)
                + jnp.dot(x_hi, w_lo, preferred_element_type=F32))

    prev_row = [cr_ref[0:1, :], ck_ref[0:1, :], cv_ref[0:1, :], cla_ref[0:1, :]]

    def prep(c, out):
        rows = slice(c * L, (c + 1) * L)

        def lerp(x_ref, idx, mu_ref):
            x = x_ref[rows, :]
            prev = _shift_rows(x, prev_row[idx])
            prev_row[idx] = x[L - 1:L, :]
            return x + mu_ref[...] * (prev - x)

        r = lerp(pr_ref, 0, mur_ref)
        k = lerp(pk_ref, 1, muk_ref)
        v = lerp(pv_ref, 2, muv_ref)
        la = lerp(la_ref, 3, mula_ref)
        xw = jnp.where(lo_half, jnp.tanh(la), 0.0)
        xa = jnp.where(lo_half, 0.0, la)
        zw = -(w0_ref[...] + dot3(xw))
        softplus = jnp.maximum(zw, 0.0) + jnp.log(1.0 + jnp.exp(-jnp.abs(zw)))
        lw = -jnp.exp(-softplus - 0.5)
        a_sig = jax.nn.sigmoid(a0_ref[...] + dot3(xa))
        yield
        l_hi, l_mid, l_lo = _split3(lw)
        cum = (jnp.dot(tri, l_hi, preferred_element_type=F32)
               + jnp.dot(tri, l_mid, preferred_element_type=F32)
               + jnp.dot(tri, l_lo, preferred_element_type=F32))
        e_pos = jnp.exp(cum)
        e_neg = jnp.exp(-cum)
        e_prev = jnp.exp(cum - lw)
        yield
        for p, sl in enumerate(pair_lanes):
            kk_raw = k[:, sl] * kk_ref[:, sl]
            kmod = k[:, sl] * (1.0 + (a_sig[:, sl] - 1.0) * ka_ref[:, sl])
            sums = _seg_sum(jnp.concatenate([kk_raw * kk_raw, r[:, sl] * kmod * rk_ref[:, sl]], axis=0), seg_ones)
            kk = kk_raw * lax.rsqrt(jnp.maximum(sums[:L], 1e-24))
            bt = (kk * a_sig[:, sl]) * e_neg[:, sl]
            kt = kmod * e_neg[:, sl]
            out.append(dict(
                at=(-kk) * e_prev[:, sl], rt=r[:, sl] * e_pos[:, sl], bt=bt, kt=kt,
                bks=jnp.concatenate([stack_heads(bt), stack_heads(kt)], axis=0),
                vb=v[:, sl].astype(BF16), vs=stack_heads(v[:, sl]),
                bonus=sums[L:] * v[:, sl], decay=e_pos[L - 1:L, sl]))
            if p % 2 == 1:
                yield

    def scores(ops):
        for d in ops:
            d["sm"] = jnp.where(tri_mask, _bdot_nt(jnp.concatenate([d["at"], d["rt"]], axis=0), d["bks"]),
                                0.0).astype(BF16)
        for d in ops:
            d["av"] = jnp.dot(d["sm"][:, 2 * L:], d["vs"], preferred_element_type=F32)
            d["x"] = jnp.concatenate([d["at"], d["av"][:L]], axis=1)
            d["pw"] = d["sm"][:L, :2 * L]

    def post(c, ops, o2):
        rows = slice(c * L, (c + 1) * L)
        for p, sl in enumerate(pair_lanes):
            mean = _seg_sum(o2[p], seg_ones) * inv_n
            dev = o2[p] - mean
            var = _seg_sum(dev * dev, seg_ones) * inv_n
            on = dev * lax.rsqrt(var + GN_EPS) * lnw_ref[:, sl] + lnb_ref[:, sl]
            o_ref[rows, sl] = ((on + ops[p]["bonus"]) * _silu(pz_ref[rows, sl])).astype(o_ref.dtype)
            if p % 2 == 1:
                yield

    st = [st_ref[p] for p in range(n_pairs)]
    ops = [[] for _ in range(n_chunks)]
    _drain(prep(0, ops[0]))
    scores(ops[0])
    side = []
    for c in range(n_chunks):
        cur = ops[c]
        if c + 1 < n_chunks:
            side.append(prep(c + 1, ops[c + 1]))
        n = 1
        while n < L:
            for d in cur:
                d["x"] = d["x"] + jnp.dot(d["pw"], stack_heads(d["x"]), preferred_element_type=F32)
            n *= 2
            if n < L:
                for d in cur:
                    d["pw"] = jnp.dot(d["pw"], stack_heads(d["pw"]), preferred_element_type=F32).astype(BF16)
            for gen in side:
                next(gen, None)
        for gen in side:
            _drain(gen)
        side = []
        if c + 1 < n_chunks:
            scores(ops[c + 1])
        hs = [_bdot_nt(jnp.concatenate([d["x"][:, :LANES], d["rt"]], axis=0), st[p]) for p, d in enumerate(cur)]
        us = [hs[p][:L] + d["x"][:, LANES:] for p, d in enumerate(cur)]
        o2 = [hs[p][L:] + jnp.dot(d["sm"][L:, :2 * L], stack_heads(us[p]), preferred_element_type=F32)
              + d["av"][L:] for p, d in enumerate(cur)]
        for p, d in enumerate(cur):
            upd = _bdot_tn(jnp.concatenate([us[p].astype(BF16), d["vb"]], axis=0),
                           jnp.concatenate([d["bt"], d["kt"]], axis=0))
            st[p] = (st[p] + jnp.where(same_head, upd, 0.0)) * d["decay"]
        side.append(post(c, cur, o2))
    for gen in side:
        _drain(gen)

    for p in range(n_pairs):
        st_ref[p] = st[p]
    cr_ref[0:1, :], ck_ref[0:1, :], cv_ref[0:1, :], cla_ref[0:1, :] = prev_row


def _rwkv(proj3, lora3, mu_rkv, mu_la, w0, a0, w2a2, k_k, k_a, r_k, ln_w, ln_b, tokens_per_step):
    bsz, seq, _ = proj3.shape
    tb = tokens_per_step

    def col(group):
        return pl.BlockSpec((None, tb, B_WIDTH), lambda b, c: (b, c, group))

    def vec(n):
        return pl.BlockSpec((1, n), lambda b, c: (0, 0))

    row = lambda t: t.reshape(1, -1)
    return pl.pallas_call(
        _rwkv_kernel,
        grid=(bsz, seq // tb),
        in_specs=[
            col(COL_BR), col(COL_BK), col(COL_BV), col(COL_BZ),
            pl.BlockSpec((None, tb, 2 * LORA), lambda b, c: (b, c, 0)),
            vec(B_WIDTH), vec(B_WIDTH), vec(B_WIDTH), vec(2 * LORA), vec(B_WIDTH), vec(B_WIDTH),
            pl.BlockSpec((2 * LORA, B_WIDTH), lambda b, c: (0, 0)),
            vec(B_WIDTH), vec(B_WIDTH), vec(B_WIDTH), vec(B_WIDTH), vec(B_WIDTH),
        ],
        out_specs=pl.BlockSpec((None, tb, B_WIDTH), lambda b, c: (b, c, 0)),
        out_shape=jax.ShapeDtypeStruct((bsz, seq, B_WIDTH), BF16),
        scratch_shapes=[
            pltpu.VMEM((B_HEADS // 2, LANES, LANES), F32),
            pltpu.VMEM((8, B_WIDTH), F32), pltpu.VMEM((8, B_WIDTH), F32), pltpu.VMEM((8, B_WIDTH), F32),
            pltpu.VMEM((8, 2 * LORA), F32),
        ],
        compiler_params=pltpu.CompilerParams(
            dimension_semantics=("parallel", "arbitrary"), vmem_limit_bytes=VMEM_LIMIT),
        name="rwkv",
    )(proj3, proj3, proj3, proj3, lora3,
      row(mu_rkv[0]), row(mu_rkv[1]), row(mu_rkv[2]), row(mu_la), row(w0), row(a0), w2a2,
      row(k_k), row(k_a), row(r_k), row(ln_w), row(ln_b))


def _merge_out_kernel(ya_ref, yb_ref, yc_ref, ga_ref, gb_ref, gc_ref, wa_ref, wb_ref, wc_ref, wo_ref, x_ref,
                      o_ref):
    merged = (jax.nn.sigmoid(ga_ref[...]) * jnp.dot(ya_ref[...], wa_ref[...], preferred_element_type=F32)
              + jax.nn.sigmoid(gb_ref[...]) * jnp.dot(yb_ref[...], wb_ref[...], preferred_element_type=F32)
              + jax.nn.sigmoid(gc_ref[...]) * jnp.dot(yc_ref[...], wc_ref[...], preferred_element_type=F32))
    o_ref[...] = x_ref[...] + jnp.dot(merged.astype(BF16), wo_ref[...], preferred_element_type=F32)


def _merge_out(ya, yb, yc, proj2, wa, wb, wc, wo, x2d, tm):
    m = x2d.shape[0]

    def ybranch(width):
        return pl.BlockSpec((tm, width), lambda i: (i, 0))

    def gate(group):
        return pl.BlockSpec((tm, D_MODEL), lambda i: (i, group * 1024 // D_MODEL))

    def resident(shape):
        return pl.BlockSpec(shape, lambda i: (0, 0), pipeline_mode=pl.Buffered(1))

    return pl.pallas_call(
        _merge_out_kernel,
        grid=(m // tm,),
        in_specs=[
            ybranch(A_WIDTH), ybranch(B_WIDTH), ybranch(C_WIDTH),
            gate(COL_GA), gate(COL_GB), gate(COL_GC),
            resident((A_WIDTH, D_MODEL)), resident((B_WIDTH, D_MODEL)), resident((C_WIDTH, D_MODEL)),
            resident((D_MODEL, D_MODEL)),
            pl.BlockSpec((tm, D_MODEL), lambda i: (i, 0)),
        ],
        out_specs=pl.BlockSpec((tm, D_MODEL), lambda i: (i, 0)),
        out_shape=jax.ShapeDtypeStruct((m, D_MODEL), F32),
        compiler_params=pltpu.CompilerParams(
            dimension_semantics=("parallel",), vmem_limit_bytes=VMEM_LIMIT),
        name="merge_out",
    )(ya, yb, yc, proj2, proj2, proj2, wa, wb, wc, wo, x2d)


def _layer(x, mem, norm_g, w_in, a_q_g, a_k_g, a_rel_bias, w_up_a,
           b_mu_rkv, b_mu_w, b_mu_a, b_w0, b_w2, b_a0, b_a2, b_k_k, b_k_a, b_r_k,
           b_ln_w, b_ln_b, w_up_b, mem_norm_g, w_mem_kv, c_q_g, c_k_g, w_up_c, w_o):
    bsz, seq, d = x.shape
    t = bsz * seq
    x2d = x.reshape(t, d)
    w_main, w_lora = _prep_w_in(w_in, 128)

    proj, lora = _in_proj(x2d, norm_g, w_main, w_lora, 1024, 1024)
    mkv = _norm_matmul(mem.reshape(bsz * N_MEM, d), mem_norm_g, w_mem_kv.astype(BF16), 1024, 1024, "mem_kv")

    proj3 = proj.reshape(bsz, seq, MAIN_COLS)
    ya = _band_attn(proj3, _band_bias_table(a_rel_bias), a_q_g, a_k_g)
    yc = _mem_attn(proj3, mkv.reshape(bsz, N_MEM, 2 * C_WIDTH), c_q_g, c_k_g, 512)
    yb = _rwkv(proj3, lora.reshape(bsz, seq, 2 * LORA), b_mu_rkv,
               jnp.concatenate([b_mu_w, b_mu_a]), b_w0, b_a0,
               jnp.concatenate([b_w2, b_a2], axis=0), b_k_k, b_k_a, b_r_k.reshape(-1), b_ln_w, b_ln_b, 4 * CHUNK)

    out = _merge_out(ya.reshape(t, A_WIDTH), yb.reshape(t, B_WIDTH), yc.reshape(t, C_WIDTH), proj,
                     w_up_a.astype(BF16), w_up_b.astype(BF16), w_up_c.astype(BF16), w_o.astype(BF16),
                     x2d, 256)
    return out.reshape(bsz, seq, d)


def kernel(x, mem, norm_g, w_in, a_q_g, a_k_g, a_rel_bias, w_up_a, b_mu_rkv, b_mu_w, b_mu_a, b_w0, b_w2, b_a0, b_a2, b_k_k, b_k_a, b_r_k, b_ln_w, b_ln_b, w_up_b, mem_norm_g, w_mem_kv, c_q_g, c_k_g, w_up_c, w_o):
    for l in range(norm_g.shape[0]):
        x = _layer(x, mem, norm_g[l], w_in[l], a_q_g[l], a_k_g[l], a_rel_bias[l], w_up_a[l],
                   b_mu_rkv[l], b_mu_w[l], b_mu_a[l], b_w0[l], b_w2[l], b_a0[l], b_a2[l],
                   b_k_k[l], b_k_a[l], b_r_k[l], b_ln_w[l], b_ln_b[l], w_up_b[l],
                   mem_norm_g[l], w_mem_kv[l], c_q_g[l], c_k_g[l], w_up_c[l], w_o[l])
    return x
```

```python
import numpy as np
import jax
import jax.numpy as jnp
from jax import lax
from jax.experimental import pallas as pl
from jax.experimental.pallas import tpu as pltpu

D_MODEL = 2048
CHUNK = 64
N_MEM = 256
NORM_EPS = 1e-6
NEG_INF = -1e30

A_HEADS = 8
A_HEAD_DIM = 128
A_WIDTH = A_HEADS * A_HEAD_DIM
A_LEFT_CHUNKS = 8
REL_CLIP = 128

B_HEADS = 16
B_HEAD_DIM = 64
B_WIDTH = B_HEADS * B_HEAD_DIM
LORA = 64
GN_EPS = 64e-5

C_HEADS = 4
C_HEAD_DIM = 256
C_WIDTH = C_HEADS * C_HEAD_DIM

LANES = 128
VMEM_LIMIT = 56 * 1024 * 1024

A_BAND = (A_LEFT_CHUNKS + 1) * CHUNK
K_WINDOW = -(-A_BAND // LANES) * LANES
K_SLACK = K_WINDOW - A_BAND
K_PAD = K_WINDOW - CHUNK
ATTN_CHUNKS_IN_FLIGHT = 8

COL_AQ, COL_AK, COL_AV, COL_AZ = 0, 1, 2, 3
COL_BR, COL_BK, COL_BV, COL_BZ = 4, 5, 6, 7
COL_CQ, COL_CZ = 8, 9
COL_GA, COL_GB, COL_GC = 10, 12, 14
MAIN_COLS = 16 * 1024
LORA_LO = 4 * A_WIDTH + 4 * B_WIDTH
LORA_HI = LORA_LO + 2 * LORA

BF16 = jnp.bfloat16
F32 = jnp.float32


def _bdot(a, b):
    return jnp.dot(a.astype(BF16), b.astype(BF16), preferred_element_type=F32)


def _bdot_nt(a, b):
    return lax.dot_general(a.astype(BF16), b.astype(BF16), (((1,), (1,)), ((), ())),
                           preferred_element_type=F32)


def _bdot_tn(a, b):
    return lax.dot_general(a.astype(BF16), b.astype(BF16), (((0,), (0,)), ((), ())),
                           preferred_element_type=F32)


def _split2(x):
    hi = x.astype(BF16)
    lo = (x - hi.astype(F32)).astype(BF16)
    return hi, lo


def _split3(x):
    hi = x.astype(BF16)
    r1 = x - hi.astype(F32)
    mid = r1.astype(BF16)
    lo = (r1 - mid.astype(F32)).astype(BF16)
    return hi, mid, lo


def _silu(x):
    return x * jax.nn.sigmoid(x)


def _prep_w_in_kernel(w_ref, main_ref, lora_ref):
    main_ref[:, :LORA_LO] = w_ref[:, :LORA_LO].astype(BF16)
    main_ref[:, LORA_LO:] = w_ref[:, LORA_HI:].astype(BF16)
    lora_ref[...] = w_ref[:, LORA_LO:LORA_HI].astype(BF16)


def _prep_w_in(w_in, rows):
    k, n = w_in.shape
    return pl.pallas_call(
        _prep_w_in_kernel,
        grid=(k // rows,),
        in_specs=[pl.BlockSpec((rows, n), lambda i: (i, 0))],
        out_specs=[
            pl.BlockSpec((rows, MAIN_COLS), lambda i: (i, 0)),
            pl.BlockSpec((rows, 2 * LORA), lambda i: (i, 0)),
        ],
        out_shape=[
            jax.ShapeDtypeStruct((k, MAIN_COLS), BF16),
            jax.ShapeDtypeStruct((k, 2 * LORA), BF16),
        ],
        compiler_params=pltpu.CompilerParams(
            dimension_semantics=("parallel",), vmem_limit_bytes=VMEM_LIMIT),
        name="prep_w_in",
    )(w_in)


def _norm_matmul_kernel(x_ref, g_ref, w_ref, o_ref, h_ref):
    @pl.when(pl.program_id(1) == 0)
    def _():
        x = x_ref[...]
        ms = jnp.mean(x * x, axis=-1, keepdims=True)
        h_ref[...] = (x * lax.rsqrt(ms + NORM_EPS) * g_ref[...]).astype(BF16)

    o_ref[...] = jnp.dot(h_ref[...], w_ref[...], preferred_element_type=F32).astype(o_ref.dtype)


def _norm_matmul(x2d, g, w_bf16, tm, tn, name):
    m, k = x2d.shape
    n = w_bf16.shape[1]
    return pl.pallas_call(
        _norm_matmul_kernel,
        grid=(m // tm, n // tn),
        in_specs=[
            pl.BlockSpec((tm, k), lambda i, j: (i, 0)),
            pl.BlockSpec((1, k), lambda i, j: (0, 0)),
            pl.BlockSpec((k, tn), lambda i, j: (0, j)),
        ],
        out_specs=pl.BlockSpec((tm, tn), lambda i, j: (i, j)),
        out_shape=jax.ShapeDtypeStruct((m, n), F32),
        scratch_shapes=[pltpu.VMEM((tm, k), BF16)],
        compiler_params=pltpu.CompilerParams(
            dimension_semantics=("parallel", "arbitrary"), vmem_limit_bytes=VMEM_LIMIT),
        name=name,
    )(x2d, g.reshape(1, k), w_bf16)


def _in_proj_kernel(x_ref, g_ref, w_ref, wla_ref, o_ref, la_ref, h_ref):
    @pl.when(pl.program_id(1) == 0)
    def _():
        x = x_ref[...]
        ms = jnp.mean(x * x, axis=-1, keepdims=True)
        h = (x * lax.rsqrt(ms + NORM_EPS) * g_ref[...]).astype(BF16)
        h_ref[...] = h
        la_ref[...] = jnp.dot(h, wla_ref[...], preferred_element_type=F32)

    o_ref[...] = jnp.dot(h_ref[...], w_ref[...], preferred_element_type=F32)


def _in_proj(x2d, g, w_main, w_lora, tm, tn):
    m, k = x2d.shape
    n = w_main.shape[1]
    return pl.pallas_call(
        _in_proj_kernel,
        grid=(m // tm, n // tn),
        in_specs=[
            pl.BlockSpec((tm, k), lambda i, j: (i, 0)),
            pl.BlockSpec((1, k), lambda i, j: (0, 0)),
            pl.BlockSpec((k, tn), lambda i, j: (0, j)),
            pl.BlockSpec((k, 2 * LORA), lambda i, j: (0, 0)),
        ],
        out_specs=[
            pl.BlockSpec((tm, tn), lambda i, j: (i, j)),
            pl.BlockSpec((tm, 2 * LORA), lambda i, j: (i, 0)),
        ],
        out_shape=[
            jax.ShapeDtypeStruct((m, n), F32),
            jax.ShapeDtypeStruct((m, 2 * LORA), F32),
        ],
        scratch_shapes=[pltpu.VMEM((tm, k), BF16)],
        compiler_params=pltpu.CompilerParams(
            dimension_semantics=("parallel", "arbitrary"), vmem_limit_bytes=VMEM_LIMIT),
        name="in_proj",
    )(x2d, g.reshape(1, k), w_main, w_lora)


def _band_attn_kernel(q_ref, k_ref, v_ref, z_ref, bias_ref, gq_ref, gk_ref, o_ref,
                      qn_ref, kp_ref, vp_ref):
    seq = q_ref.shape[0]
    scale = A_HEAD_DIM ** -0.5
    q = q_ref[...]
    qn = q * lax.rsqrt(jnp.mean(q * q, axis=-1, keepdims=True) + NORM_EPS) * gq_ref[...]
    qn_ref[...] = (qn * scale).astype(BF16)
    k = k_ref[...]
    kn = k * lax.rsqrt(jnp.mean(k * k, axis=-1, keepdims=True) + NORM_EPS) * gk_ref[...]
    kp_ref[0:K_PAD, :] = jnp.zeros((K_PAD, A_HEAD_DIM), BF16)
    vp_ref[0:K_PAD, :] = jnp.zeros((K_PAD, A_HEAD_DIM), BF16)
    kp_ref[K_PAD:, :] = kn.astype(BF16)
    vp_ref[K_PAD:, :] = v_ref[...].astype(BF16)
    def scores(c):
        q0 = c * CHUNK
        first = K_PAD - q0
        skip = max(first, 0) // LANES * LANES
        s = _bdot_nt(qn_ref[q0:q0 + CHUNK, :], kp_ref[q0 + skip:q0 + K_WINDOW, :]) + bias_ref[:, skip:]
        if first > skip:
            col = lax.broadcasted_iota(jnp.int32, s.shape, 1)
            s = jnp.where(col >= first - skip, s, NEG_INF)
        return s, skip

    for c0 in range(0, seq // CHUNK, ATTN_CHUNKS_IN_FLIGHT):
        group = range(c0, c0 + ATTN_CHUNKS_IN_FLIGHT)
        sc = [scores(c) for c in group]
        mx = [jnp.max(s, axis=-1, keepdims=True) for s, _ in sc]
        ps = [jnp.exp(s - m) for (s, _), m in zip(sc, mx)]
        ls = [jnp.sum(p, axis=-1, keepdims=True) for p in ps]
        outs = [_bdot(p, vp_ref[c * CHUNK + skip:c * CHUNK + K_WINDOW, :]) / l
                for c, p, l, (_, skip) in zip(group, ps, ls, sc)]
        for c, o in zip(group, outs):
            q0 = c * CHUNK
            o_ref[q0:q0 + CHUNK, :] = (o * _silu(z_ref[q0:q0 + CHUNK, :])).astype(o_ref.dtype)


def _band_bias_table(rel_bias):
    period = 1024
    q = np.arange(period)
    m_minus_r = np.where(q < K_WINDOW, q, q - period)
    dist = K_PAD - m_minus_r
    f = rel_bias[:, np.clip(dist, -REL_CLIP, REL_CLIP) + REL_CLIP].astype(F32)
    seq = jnp.tile(f, (1, CHUNK))[:, :CHUNK * (period - 1)]
    toeplitz = seq.reshape(-1, CHUNK, period - 1)[:, :, :K_WINDOW]
    in_band = np.arange(K_WINDOW)[None, None, :] >= K_SLACK
    return jnp.where(jnp.asarray(in_band), toeplitz, NEG_INF)


def _band_attn(proj3, bias_tab, gq, gk):
    bsz, seq, _ = proj3.shape
    hb = A_WIDTH // A_HEAD_DIM

    def col(group):
        return pl.BlockSpec((None, seq, A_HEAD_DIM), lambda b, h: (b, 0, group * hb + h))

    return pl.pallas_call(
        _band_attn_kernel,
        grid=(bsz, A_HEADS),
        in_specs=[
            col(COL_AQ), col(COL_AK), col(COL_AV), col(COL_AZ),
            pl.BlockSpec((None, CHUNK, K_WINDOW), lambda b, h: (h, 0, 0)),
            pl.BlockSpec((1, A_HEAD_DIM), lambda b, h: (0, 0)),
            pl.BlockSpec((1, A_HEAD_DIM), lambda b, h: (0, 0)),
        ],
        out_specs=pl.BlockSpec((None, seq, A_HEAD_DIM), lambda b, h: (b, 0, h)),
        out_shape=jax.ShapeDtypeStruct((bsz, seq, A_WIDTH), BF16),
        scratch_shapes=[
            pltpu.VMEM((seq, A_HEAD_DIM), BF16),
            pltpu.VMEM((seq + K_PAD, A_HEAD_DIM), BF16),
            pltpu.VMEM((seq + K_PAD, A_HEAD_DIM), BF16),
        ],
        compiler_params=pltpu.CompilerParams(
            dimension_semantics=("parallel", "parallel"), vmem_limit_bytes=VMEM_LIMIT),
        name="band_attn",
    )(proj3, proj3, proj3, proj3, bias_tab, gq.reshape(1, -1), gk.reshape(1, -1))


def _mem_attn_kernel(q_ref, z_ref, mk_ref, mv_ref, gq_ref, gk_ref, o_ref):
    scale = C_HEAD_DIM ** -0.5
    heads = [slice(h * C_HEAD_DIM, (h + 1) * C_HEAD_DIM) for h in range(C_HEADS)]

    def rms(x, g_ref):
        return x * lax.rsqrt(jnp.mean(x * x, axis=-1, keepdims=True) + NORM_EPS) * g_ref[...]

    qn = [rms(q_ref[:, sl], gq_ref) * scale for sl in heads]
    kn = [rms(mk_ref[:, sl], gk_ref) for sl in heads]
    sc = [_bdot_nt(q, k) for q, k in zip(qn, kn)]
    mx = [jnp.max(s, axis=-1, keepdims=True) for s in sc]
    ps = [jnp.exp(s - m) for s, m in zip(sc, mx)]
    ls = [jnp.sum(p, axis=-1, keepdims=True) for p in ps]
    outs = [_bdot(p, mv_ref[:, sl]) / l for p, l, sl in zip(ps, ls, heads)]
    for o, sl in zip(outs, heads):
        o_ref[:, sl] = (o * _silu(z_ref[:, sl])).astype(o_ref.dtype)


def _mem_attn(proj3, mkv3, gq, gk, ts):
    bsz, seq, _ = proj3.shape
    return pl.pallas_call(
        _mem_attn_kernel,
        grid=(bsz, seq // ts),
        in_specs=[
            pl.BlockSpec((None, ts, C_WIDTH), lambda b, s: (b, s, COL_CQ)),
            pl.BlockSpec((None, ts, C_WIDTH), lambda b, s: (b, s, COL_CZ)),
            pl.BlockSpec((None, N_MEM, C_WIDTH), lambda b, s: (b, 0, 0)),
            pl.BlockSpec((None, N_MEM, C_WIDTH), lambda b, s: (b, 0, 1)),
            pl.BlockSpec((1, C_HEAD_DIM), lambda b, s: (0, 0)),
            pl.BlockSpec((1, C_HEAD_DIM), lambda b, s: (0, 0)),
        ],
        out_specs=pl.BlockSpec((None, ts, C_WIDTH), lambda b, s: (b, s, 0)),
        out_shape=jax.ShapeDtypeStruct((bsz, seq, C_WIDTH), BF16),
        compiler_params=pltpu.CompilerParams(
            dimension_semantics=("parallel", "parallel"), vmem_limit_bytes=VMEM_LIMIT),
        name="mem_attn",
    )(proj3, proj3, mkv3, mkv3, gq.reshape(1, -1), gk.reshape(1, -1))


def _shift_rows(x, carry_row):
    rolled = pltpu.roll(x, 1, 0)
    row = lax.broadcasted_iota(jnp.int32, x.shape, 0)
    return jnp.where(row == 0, carry_row, rolled)


def _seg_sum(x, seg_ones2):
    hi, lo = _split2(x)
    return jnp.dot(jnp.concatenate([hi, lo], axis=1), seg_ones2, preferred_element_type=F32)


def _drain(gen):
    for _ in gen:
        pass


def _rwkv_kernel(pr_ref, pk_ref, pv_ref, pz_ref, la_ref,
                 mur_ref, muk_ref, muv_ref, mula_ref, w0_ref, a0_ref, w2a2_ref,
                 kk_ref, ka_ref, rk_ref, lnw_ref, lnb_ref,
                 o_ref,
                 st_ref, cr_ref, ck_ref, cv_ref, cla_ref):
    L = CHUNK
    n_chunks = pr_ref.shape[0] // L
    n_pairs = B_HEADS // 2
    pair_lanes = [slice(p * LANES, (p + 1) * LANES) for p in range(n_pairs)]

    @pl.when(pl.program_id(1) == 0)
    def _():
        st_ref[...] = jnp.zeros_like(st_ref)
        cr_ref[...] = jnp.zeros_like(cr_ref)
        ck_ref[...] = jnp.zeros_like(ck_ref)
        cv_ref[...] = jnp.zeros_like(cv_ref)
        cla_ref[...] = jnp.zeros_like(cla_ref)

    lane = lax.broadcasted_iota(jnp.int32, (L, LANES), 1)
    lo_half = lane < B_HEAD_DIM
    lo_half2 = (lax.broadcasted_iota(jnp.int32, (L, 2 * LANES), 1) & B_HEAD_DIM) == 0
    ti = lax.broadcasted_iota(jnp.int32, (L, L), 0)
    si = lax.broadcasted_iota(jnp.int32, (L, L), 1)
    tri = (si <= ti).astype(BF16)
    tri3 = jnp.concatenate([tri, tri, tri], axis=1)
    ri = lax.broadcasted_iota(jnp.int32, (LANES, LANES), 0)
    ci = lax.broadcasted_iota(jnp.int32, (LANES, LANES), 1)
    same_head = (ri < B_HEAD_DIM) == (ci < B_HEAD_DIM)
    seg_ones = jnp.concatenate([same_head.astype(BF16)] * 2, axis=0)
    inv_n = 1.0 / B_HEAD_DIM
    mi = lax.broadcasted_iota(jnp.int32, (2 * L, 4 * L), 0)
    mj = lax.broadcasted_iota(jnp.int32, (2 * L, 4 * L), 1)
    tt, ss = mi & (L - 1), mj & (L - 1)
    tri_mask = (ss < tt) | ((mi >= L) & (ss == tt))
    w_hi, w_lo = _split2(w2a2_ref[...])
    w_hhl = jnp.concatenate([w_hi, w_hi, w_lo], axis=0)

    def stack_heads(x2):
        lo = lo_half if x2.shape[1] == LANES else lo_half2
        xb = x2.astype(BF16)
        return jnp.concatenate([jnp.where(lo, xb, 0.0), jnp.where(lo, 0.0, xb)], axis=0).astype(BF16)

    def dot3(x):
        x_hi, x_lo = _split2(x)
        return jnp.dot(jnp.concatenate([x_hi, x_lo, x_hi], axis=1), w_hhl, preferred_element_type=F32)

    prev_row = [cr_ref[0:1, :], ck_ref[0:1, :], cv_ref[0:1, :], cla_ref[0:1, :]]

    def prep(c, out):
        rows = slice(c * L, (c + 1) * L)

        def lerp(x_ref, idx, mu_ref):
            x = x_ref[rows, :]
            prev = _shift_rows(x, prev_row[idx])
            prev_row[idx] = x[L - 1:L, :]
            return x + mu_ref[...] * (prev - x)

        r = lerp(pr_ref, 0, mur_ref)
        k = lerp(pk_ref, 1, muk_ref)
        v = lerp(pv_ref, 2, muv_ref)
        la = lerp(la_ref, 3, mula_ref)
        xw = jnp.where(lo_half, jnp.tanh(la), 0.0)
        xa = jnp.where(lo_half, 0.0, la)
        zw = -(w0_ref[...] + dot3(xw))
        softplus = jnp.maximum(zw, 0.0) + jnp.log(1.0 + jnp.exp(-jnp.abs(zw)))
        lw = -jnp.exp(-softplus - 0.5)
        a_sig = jax.nn.sigmoid(a0_ref[...] + dot3(xa))
        yield
        cum = jnp.dot(tri3, jnp.concatenate(_split3(lw), axis=0), preferred_element_type=F32)
        e_pos = jnp.exp(cum)
        e_neg = jnp.exp(-cum)
        e_prev = jnp.exp(cum - lw)
        yield
        for p, sl in enumerate(pair_lanes):
            kk_raw = k[:, sl] * kk_ref[:, sl]
            kmod = k[:, sl] * (1.0 + (a_sig[:, sl] - 1.0) * ka_ref[:, sl])
            sums = _seg_sum(jnp.concatenate([kk_raw * kk_raw, r[:, sl] * kmod * rk_ref[:, sl]], axis=0), seg_ones)
            kk = kk_raw * lax.rsqrt(jnp.maximum(sums[:L], 1e-24))
            bt = (kk * a_sig[:, sl]) * e_neg[:, sl]
            kt = kmod * e_neg[:, sl]
            out.append(dict(
                at=(-kk) * e_prev[:, sl], rt=r[:, sl] * e_pos[:, sl], bt=bt, kt=kt,
                bks=jnp.concatenate([stack_heads(bt), stack_heads(kt)], axis=0),
                vb=v[:, sl].astype(BF16), vs=stack_heads(v[:, sl]),
                bonus=sums[L:] * v[:, sl], decay=e_pos[L - 1:L, sl]))
            if p % 2 == 1:
                yield

    def scores(ops):
        for d in ops:
            d["sm"] = jnp.where(tri_mask, _bdot_nt(jnp.concatenate([d["at"], d["rt"]], axis=0), d["bks"]),
                                0.0).astype(BF16)
        for d in ops:
            d["av"] = jnp.dot(d["sm"][:, 2 * L:], d["vs"], preferred_element_type=F32)
            d["x"] = jnp.concatenate([d["at"], d["av"][:L]], axis=1)
            d["pw"] = d["sm"][:L, :2 * L]

    def post(c, ops, o2):
        rows = slice(c * L, (c + 1) * L)
        for p, sl in enumerate(pair_lanes):
            mean = _seg_sum(o2[p], seg_ones) * inv_n
            dev = o2[p] - mean
            var = _seg_sum(dev * dev, seg_ones) * inv_n
            on = dev * lax.rsqrt(var + GN_EPS) * lnw_ref[:, sl] + lnb_ref[:, sl]
            o_ref[rows, sl] = ((on + ops[p]["bonus"]) * _silu(pz_ref[rows, sl])).astype(o_ref.dtype)
            if p % 2 == 1:
                yield

    st = [st_ref[p] for p in range(n_pairs)]
    ops = [[] for _ in range(n_chunks)]
    _drain(prep(0, ops[0]))
    scores(ops[0])
    side = []
    for c in range(n_chunks):
        cur = ops[c]
        if c + 1 < n_chunks:
            side.append(prep(c + 1, ops[c + 1]))
        n = 1
        while n < L:
            for d in cur:
                d["x"] = d["x"] + jnp.dot(d["pw"], stack_heads(d["x"]), preferred_element_type=F32)
            n *= 2
            if n < L:
                for d in cur:
                    d["pw"] = jnp.dot(d["pw"], stack_heads(d["pw"]), preferred_element_type=F32).astype(BF16)
            for gen in side:
                next(gen, None)
        for gen in side:
            _drain(gen)
        side = []
        if c + 1 < n_chunks:
            scores(ops[c + 1])
        hs = [_bdot_nt(jnp.concatenate([d["x"][:, :LANES], d["rt"]], axis=0), st[p]) for p, d in enumerate(cur)]
        us = [hs[p][:L] + d["x"][:, LANES:] for p, d in enumerate(cur)]
        o2 = [hs[p][L:] + jnp.dot(d["sm"][L:, :2 * L], stack_heads(us[p]), preferred_element_type=F32)
              + d["av"][L:] for p, d in enumerate(cur)]
        for p, d in enumerate(cur):
            upd = _bdot_tn(jnp.concatenate([us[p].astype(BF16), d["vb"]], axis=0),
                           jnp.concatenate([d["bt"], d["kt"]], axis=0))
            st[p] = (st[p] + jnp.where(same_head, upd, 0.0)) * d["decay"]
        side.append(post(c, cur, o2))
    for gen in side:
        _drain(gen)

    for p in range(n_pairs):
        st_ref[p] = st[p]
    cr_ref[0:1, :], ck_ref[0:1, :], cv_ref[0:1, :], cla_ref[0:1, :] = prev_row


def _rwkv(proj3, lora3, mu_rkv, mu_la, w0, a0, w2a2, k_k, k_a, r_k, ln_w, ln_b, tokens_per_step):
    bsz, seq, _ = proj3.shape
    tb = tokens_per_step

    def col(group):
        return pl.BlockSpec((None, tb, B_WIDTH), lambda b, c: (b, c, group))

    def vec(n):
        return pl.BlockSpec((1, n), lambda b, c: (0, 0))

    row = lambda t: t.reshape(1, -1)
    return pl.pallas_call(
        _rwkv_kernel,
        grid=(bsz, seq // tb),
        in_specs=[
            col(COL_BR), col(COL_BK), col(COL_BV), col(COL_BZ),
            pl.BlockSpec((None, tb, 2 * LORA), lambda b, c: (b, c, 0)),
            vec(B_WIDTH), vec(B_WIDTH), vec(B_WIDTH), vec(2 * LORA), vec(B_WIDTH), vec(B_WIDTH),
            pl.BlockSpec((2 * LORA, B_WIDTH), lambda b, c: (0, 0)),
            vec(B_WIDTH), vec(B_WIDTH), vec(B_WIDTH), vec(B_WIDTH), vec(B_WIDTH),
        ],
        out_specs=pl.BlockSpec((None, tb, B_WIDTH), lambda b, c: (b, c, 0)),
        out_shape=jax.ShapeDtypeStruct((bsz, seq, B_WIDTH), BF16),
        scratch_shapes=[
            pltpu.VMEM((B_HEADS // 2, LANES, LANES), F32),
            pltpu.VMEM((8, B_WIDTH), F32), pltpu.VMEM((8, B_WIDTH), F32), pltpu.VMEM((8, B_WIDTH), F32),
            pltpu.VMEM((8, 2 * LORA), F32),
        ],
        compiler_params=pltpu.CompilerParams(
            dimension_semantics=("parallel", "arbitrary"), vmem_limit_bytes=VMEM_LIMIT),
        name="rwkv",
    )(proj3, proj3, proj3, proj3, lora3,
      row(mu_rkv[0]), row(mu_rkv[1]), row(mu_rkv[2]), row(mu_la), row(w0), row(a0), w2a2,
      row(k_k), row(k_a), row(r_k), row(ln_w), row(ln_b))


def _merge_out_kernel(ya_ref, yb_ref, yc_ref, ga_ref, gb_ref, gc_ref, wa_ref, wb_ref, wc_ref, wo_ref, x_ref,
                      o_ref):
    merged = (jax.nn.sigmoid(ga_ref[...]) * jnp.dot(ya_ref[...], wa_ref[...], preferred_element_type=F32)
              + jax.nn.sigmoid(gb_ref[...]) * jnp.dot(yb_ref[...], wb_ref[...], preferred_element_type=F32)
              + jax.nn.sigmoid(gc_ref[...]) * jnp.dot(yc_ref[...], wc_ref[...], preferred_element_type=F32))
    o_ref[...] = x_ref[...] + jnp.dot(merged.astype(BF16), wo_ref[...], preferred_element_type=F32)


def _merge_out(ya, yb, yc, proj2, wa, wb, wc, wo, x2d, tm):
    m = x2d.shape[0]

    def ybranch(width):
        return pl.BlockSpec((tm, width), lambda i: (i, 0))

    def gate(group):
        return pl.BlockSpec((tm, D_MODEL), lambda i: (i, group * 1024 // D_MODEL))

    def resident(shape):
        return pl.BlockSpec(shape, lambda i: (0, 0), pipeline_mode=pl.Buffered(1))

    return pl.pallas_call(
        _merge_out_kernel,
        grid=(m // tm,),
        in_specs=[
            ybranch(A_WIDTH), ybranch(B_WIDTH), ybranch(C_WIDTH),
            gate(COL_GA), gate(COL_GB), gate(COL_GC),
            resident((A_WIDTH, D_MODEL)), resident((B_WIDTH, D_MODEL)), resident((C_WIDTH, D_MODEL)),
            resident((D_MODEL, D_MODEL)),
            pl.BlockSpec((tm, D_MODEL), lambda i: (i, 0)),
        ],
        out_specs=pl.BlockSpec((tm, D_MODEL), lambda i: (i, 0)),
        out_shape=jax.ShapeDtypeStruct((m, D_MODEL), F32),
        compiler_params=pltpu.CompilerParams(
            dimension_semantics=("parallel",), vmem_limit_bytes=VMEM_LIMIT),
        name="merge_out",
    )(ya, yb, yc, proj2, proj2, proj2, wa, wb, wc, wo, x2d)


def _layer(x, mem, norm_g, w_in, a_q_g, a_k_g, a_rel_bias, w_up_a,
           b_mu_rkv, b_mu_w, b_mu_a, b_w0, b_w2, b_a0, b_a2, b_k_k, b_k_a, b_r_k,
           b_ln_w, b_ln_b, w_up_b, mem_norm_g, w_mem_kv, c_q_g, c_k_g, w_up_c, w_o):
    bsz, seq, d = x.shape
    t = bsz * seq
    x2d = x.reshape(t, d)
    w_main, w_lora = _prep_w_in(w_in, 128)

    proj, lora = _in_proj(x2d, norm_g, w_main, w_lora, 1024, 1024)
    mkv = _norm_matmul(mem.reshape(bsz * N_MEM, d), mem_norm_g, w_mem_kv.astype(BF16), 1024, 1024, "mem_kv")

    proj3 = proj.reshape(bsz, seq, MAIN_COLS)
    ya = _band_attn(proj3, _band_bias_table(a_rel_bias), a_q_g, a_k_g)
    yc = _mem_attn(proj3, mkv.reshape(bsz, N_MEM, 2 * C_WIDTH), c_q_g, c_k_g, 512)
    yb = _rwkv(proj3, lora.reshape(bsz, seq, 2 * LORA), b_mu_rkv,
               jnp.concatenate([b_mu_w, b_mu_a]), b_w0, b_a0,
               jnp.concatenate([b_w2, b_a2], axis=0), b_k_k, b_k_a, b_r_k.reshape(-1), b_ln_w, b_ln_b, 4 * CHUNK)

    out = _merge_out(ya.reshape(t, A_WIDTH), yb.reshape(t, B_WIDTH), yc.reshape(t, C_WIDTH), proj,
                     w_up_a.astype(BF16), w_up_b.astype(BF16), w_up_c.astype(BF16), w_o.astype(BF16),
                     x2d, 256)
    return out.reshape(bsz, seq, d)


def kernel(x, mem, norm_g, w_in, a_q_g, a_k_g, a_rel_bias, w_up_a, b_mu_rkv, b_mu_w, b_mu_a, b_w0, b_w2, b_a0, b_a2, b_k_k, b_k_a, b_r_k, b_ln_w, b_ln_b, w_up_b, mem_norm_g, w_mem_kv, c_q_g, c_k_g, w_up_c, w_o):
    for l in range(norm_g.shape[0]):
        x = _layer(x, mem, norm_g[l], w_in[l], a_q_g[l], a_k_g[l], a_rel_bias[l], w_up_a[l],
                   b_mu_rkv[l], b_mu_w[l], b_mu_a[l], b_w0[l], b_w2[l], b_a0[l], b_a2[l],
                   b_k_k[l], b_k_a[l], b_r_k[l], b_ln_w[l], b_ln_b[l], w_up_b[l],
                   mem_norm_g[l], w_mem_kv[l], c_q_g[l], c_k_g[l], w_up_c[l], w_o[l])
    return x
```

```python
import numpy as np
import jax
import jax.numpy as jnp
from jax import lax
from jax.experimental import pallas as pl
from jax.experimental.pallas import tpu as pltpu

D_MODEL = 2048
CHUNK = 64
N_MEM = 256
NORM_EPS = 1e-6
NEG_INF = -1e30

A_HEADS = 8
A_HEAD_DIM = 128
A_WIDTH = A_HEADS * A_HEAD_DIM
A_LEFT_CHUNKS = 8
REL_CLIP = 128

B_HEADS = 16
B_HEAD_DIM = 64
B_WIDTH = B_HEADS * B_HEAD_DIM
LORA = 64
GN_EPS = 64e-5

C_HEADS = 4
C_HEAD_DIM = 256
C_WIDTH = C_HEADS * C_HEAD_DIM

LANES = 128
VMEM_LIMIT = 56 * 1024 * 1024

A_BAND = (A_LEFT_CHUNKS + 1) * CHUNK
K_WINDOW = -(-A_BAND // LANES) * LANES
K_SLACK = K_WINDOW - A_BAND
K_PAD = K_WINDOW - CHUNK
ATTN_CHUNKS_IN_FLIGHT = 8

COL_AQ, COL_AK, COL_AV, COL_AZ = 0, 1, 2, 3
COL_BR, COL_BK, COL_BV, COL_BZ = 4, 5, 6, 7
COL_CQ, COL_CZ = 8, 9
COL_GA, COL_GB, COL_GC = 10, 12, 14
MAIN_COLS = 16 * 1024
LORA_LO = 4 * A_WIDTH + 4 * B_WIDTH
LORA_HI = LORA_LO + 2 * LORA

BF16 = jnp.bfloat16
F32 = jnp.float32


def _bdot(a, b):
    return jnp.dot(a.astype(BF16), b.astype(BF16), preferred_element_type=F32)


def _bdot_nt(a, b):
    return lax.dot_general(a.astype(BF16), b.astype(BF16), (((1,), (1,)), ((), ())),
                           preferred_element_type=F32)


def _bdot_tn(a, b):
    return lax.dot_general(a.astype(BF16), b.astype(BF16), (((0,), (0,)), ((), ())),
                           preferred_element_type=F32)


def _split2(x):
    hi = x.astype(BF16)
    lo = (x - hi.astype(F32)).astype(BF16)
    return hi, lo


def _split3(x):
    hi = x.astype(BF16)
    r1 = x - hi.astype(F32)
    mid = r1.astype(BF16)
    lo = (r1 - mid.astype(F32)).astype(BF16)
    return hi, mid, lo


def _silu(x):
    return x * jax.nn.sigmoid(x)


def _prep_w_in_kernel(w_ref, main_ref, lora_ref):
    main_ref[:, :LORA_LO] = w_ref[:, :LORA_LO].astype(BF16)
    main_ref[:, LORA_LO:] = w_ref[:, LORA_HI:].astype(BF16)
    lora_ref[...] = w_ref[:, LORA_LO:LORA_HI].astype(BF16)


def _prep_w_in(w_in, rows):
    k, n = w_in.shape
    return pl.pallas_call(
        _prep_w_in_kernel,
        grid=(k // rows,),
        in_specs=[pl.BlockSpec((rows, n), lambda i: (i, 0))],
        out_specs=[
            pl.BlockSpec((rows, MAIN_COLS), lambda i: (i, 0)),
            pl.BlockSpec((rows, 2 * LORA), lambda i: (i, 0)),
        ],
        out_shape=[
            jax.ShapeDtypeStruct((k, MAIN_COLS), BF16),
            jax.ShapeDtypeStruct((k, 2 * LORA), BF16),
        ],
        compiler_params=pltpu.CompilerParams(
            dimension_semantics=("parallel",), vmem_limit_bytes=VMEM_LIMIT),
        name="prep_w_in",
    )(w_in)


def _cast_weights_kernel(*refs):
    n = len(refs) // 2
    for src, dst in zip(refs[:n], refs[n:]):
        dst[...] = src[...].astype(BF16)


def _cast_weights(ws, n_steps):
    specs = [pl.BlockSpec((w.shape[0] // n_steps, w.shape[1]), lambda i: (i, 0)) for w in ws]
    return pl.pallas_call(
        _cast_weights_kernel,
        grid=(n_steps,),
        in_specs=specs,
        out_specs=specs,
        out_shape=[jax.ShapeDtypeStruct(w.shape, BF16) for w in ws],
        compiler_params=pltpu.CompilerParams(
            dimension_semantics=("parallel",), vmem_limit_bytes=VMEM_LIMIT),
        name="cast_weights",
    )(*ws)


def _norm_matmul_kernel(x_ref, g_ref, w_ref, o_ref, h_ref):
    @pl.when(pl.program_id(1) == 0)
    def _():
        x = x_ref[...]
        ms = jnp.mean(x * x, axis=-1, keepdims=True)
        h_ref[...] = (x * lax.rsqrt(ms + NORM_EPS) * g_ref[...]).astype(BF16)

    o_ref[...] = jnp.dot(h_ref[...], w_ref[...], preferred_element_type=F32).astype(o_ref.dtype)


def _norm_matmul(x2d, g, w_bf16, tm, tn, name):
    m, k = x2d.shape
    n = w_bf16.shape[1]
    return pl.pallas_call(
        _norm_matmul_kernel,
        grid=(m // tm, n // tn),
        in_specs=[
            pl.BlockSpec((tm, k), lambda i, j: (i, 0)),
            pl.BlockSpec((1, k), lambda i, j: (0, 0)),
            pl.BlockSpec((k, tn), lambda i, j: (0, j)),
        ],
        out_specs=pl.BlockSpec((tm, tn), lambda i, j: (i, j)),
        out_shape=jax.ShapeDtypeStruct((m, n), F32),
        scratch_shapes=[pltpu.VMEM((tm, k), BF16)],
        compiler_params=pltpu.CompilerParams(
            dimension_semantics=("parallel", "arbitrary"), vmem_limit_bytes=VMEM_LIMIT),
        name=name,
    )(x2d, g.reshape(1, k), w_bf16)


def _in_proj_kernel(x_ref, g_ref, w_ref, wla_ref, o_ref, la_ref, h_ref):
    @pl.when(pl.program_id(1) == 0)
    def _():
        x = x_ref[...]
        ms = jnp.mean(x * x, axis=-1, keepdims=True)
        h = (x * lax.rsqrt(ms + NORM_EPS) * g_ref[...]).astype(BF16)
        h_ref[...] = h
        la_ref[...] = jnp.dot(h, wla_ref[...], preferred_element_type=F32)

    o_ref[...] = jnp.dot(h_ref[...], w_ref[...], preferred_element_type=F32)


def _in_proj(x2d, g, w_main, w_lora, tm, tn):
    m, k = x2d.shape
    n = w_main.shape[1]
    return pl.pallas_call(
        _in_proj_kernel,
        grid=(m // tm, n // tn),
        in_specs=[
            pl.BlockSpec((tm, k), lambda i, j: (i, 0)),
            pl.BlockSpec((1, k), lambda i, j: (0, 0)),
            pl.BlockSpec((k, tn), lambda i, j: (0, j)),
            pl.BlockSpec((k, 2 * LORA), lambda i, j: (0, 0)),
        ],
        out_specs=[
            pl.BlockSpec((tm, tn), lambda i, j: (i, j)),
            pl.BlockSpec((tm, 2 * LORA), lambda i, j: (i, 0)),
        ],
        out_shape=[
            jax.ShapeDtypeStruct((m, n), F32),
            jax.ShapeDtypeStruct((m, 2 * LORA), F32),
        ],
        scratch_shapes=[pltpu.VMEM((tm, k), BF16)],
        compiler_params=pltpu.CompilerParams(
            dimension_semantics=("parallel", "arbitrary"), vmem_limit_bytes=VMEM_LIMIT),
        name="in_proj",
    )(x2d, g.reshape(1, k), w_main, w_lora)


def _band_attn_kernel(q_ref, k_ref, v_ref, z_ref, bias_ref, gq_ref, gk_ref, o_ref,
                      qn_ref, kp_ref, vp_ref):
    seq = q_ref.shape[0]
    scale = A_HEAD_DIM ** -0.5
    q = q_ref[...]
    qn = q * lax.rsqrt(jnp.mean(q * q, axis=-1, keepdims=True) + NORM_EPS) * gq_ref[...]
    qn_ref[...] = (qn * scale).astype(BF16)
    k = k_ref[...]
    kn = k * lax.rsqrt(jnp.mean(k * k, axis=-1, keepdims=True) + NORM_EPS) * gk_ref[...]
    kp_ref[0:K_PAD, :] = jnp.zeros((K_PAD, A_HEAD_DIM), BF16)
    vp_ref[0:K_PAD, :] = jnp.zeros((K_PAD, A_HEAD_DIM), BF16)
    kp_ref[K_PAD:, :] = kn.astype(BF16)
    vp_ref[K_PAD:, :] = v_ref[...].astype(BF16)
    def scores(c):
        q0 = c * CHUNK
        first = K_PAD - q0
        skip = max(first, 0) // LANES * LANES
        s = _bdot_nt(qn_ref[q0:q0 + CHUNK, :], kp_ref[q0 + skip:q0 + K_WINDOW, :]) + bias_ref[:, skip:]
        if first > skip:
            col = lax.broadcasted_iota(jnp.int32, s.shape, 1)
            s = jnp.where(col >= first - skip, s, NEG_INF)
        return s, skip

    for c0 in range(0, seq // CHUNK, ATTN_CHUNKS_IN_FLIGHT):
        group = range(c0, c0 + ATTN_CHUNKS_IN_FLIGHT)
        sc = [scores(c) for c in group]
        mx = [jnp.max(s, axis=-1, keepdims=True) for s, _ in sc]
        ps = [jnp.exp(s - m) for (s, _), m in zip(sc, mx)]
        ls = [jnp.sum(p, axis=-1, keepdims=True) for p in ps]
        outs = [_bdot(p, vp_ref[c * CHUNK + skip:c * CHUNK + K_WINDOW, :]) / l
                for c, p, l, (_, skip) in zip(group, ps, ls, sc)]
        for c, o in zip(group, outs):
            q0 = c * CHUNK
            o_ref[q0:q0 + CHUNK, :] = (o * _silu(z_ref[q0:q0 + CHUNK, :])).astype(o_ref.dtype)


def _band_bias_table(rel_bias):
    period = 1024
    q = np.arange(period)
    m_minus_r = np.where(q < K_WINDOW, q, q - period)
    dist = K_PAD - m_minus_r
    f = rel_bias[:, np.clip(dist, -REL_CLIP, REL_CLIP) + REL_CLIP].astype(F32)
    seq = jnp.tile(f, (1, CHUNK))[:, :CHUNK * (period - 1)]
    toeplitz = seq.reshape(-1, CHUNK, period - 1)[:, :, :K_WINDOW]
    in_band = np.arange(K_WINDOW)[None, None, :] >= K_SLACK
    return jnp.where(jnp.asarray(in_band), toeplitz, NEG_INF)


def _band_attn(proj3, bias_tab, gq, gk):
    bsz, seq, _ = proj3.shape
    hb = A_WIDTH // A_HEAD_DIM

    def col(group):
        return pl.BlockSpec((None, seq, A_HEAD_DIM), lambda b, h: (b, 0, group * hb + h))

    return pl.pallas_call(
        _band_attn_kernel,
        grid=(bsz, A_HEADS),
        in_specs=[
            col(COL_AQ), col(COL_AK), col(COL_AV), col(COL_AZ),
            pl.BlockSpec((None, CHUNK, K_WINDOW), lambda b, h: (h, 0, 0)),
            pl.BlockSpec((1, A_HEAD_DIM), lambda b, h: (0, 0)),
            pl.BlockSpec((1, A_HEAD_DIM), lambda b, h: (0, 0)),
        ],
        out_specs=pl.BlockSpec((None, seq, A_HEAD_DIM), lambda b, h: (b, 0, h)),
        out_shape=jax.ShapeDtypeStruct((bsz, seq, A_WIDTH), BF16),
        scratch_shapes=[
            pltpu.VMEM((seq, A_HEAD_DIM), BF16),
            pltpu.VMEM((seq + K_PAD, A_HEAD_DIM), BF16),
            pltpu.VMEM((seq + K_PAD, A_HEAD_DIM), BF16),
        ],
        compiler_params=pltpu.CompilerParams(
            dimension_semantics=("parallel", "parallel"), vmem_limit_bytes=VMEM_LIMIT),
        name="band_attn",
    )(proj3, proj3, proj3, proj3, bias_tab, gq.reshape(1, -1), gk.reshape(1, -1))


def _mem_attn_kernel(q_ref, z_ref, mk_ref, mv_ref, gq_ref, gk_ref, o_ref):
    scale = C_HEAD_DIM ** -0.5
    heads = [slice(h * C_HEAD_DIM, (h + 1) * C_HEAD_DIM) for h in range(C_HEADS)]

    def rms(x, g_ref):
        return x * lax.rsqrt(jnp.mean(x * x, axis=-1, keepdims=True) + NORM_EPS) * g_ref[...]

    qn = [rms(q_ref[:, sl], gq_ref) * scale for sl in heads]
    kn = [rms(mk_ref[:, sl], gk_ref) for sl in heads]
    sc = [_bdot_nt(q, k) for q, k in zip(qn, kn)]
    mx = [jnp.max(s, axis=-1, keepdims=True) for s in sc]
    ps = [jnp.exp(s - m) for s, m in zip(sc, mx)]
    ls = [jnp.sum(p, axis=-1, keepdims=True) for p in ps]
    outs = [_bdot(p, mv_ref[:, sl]) / l for p, l, sl in zip(ps, ls, heads)]
    for o, sl in zip(outs, heads):
        o_ref[:, sl] = (o * _silu(z_ref[:, sl])).astype(o_ref.dtype)


def _mem_attn(proj3, mkv3, gq, gk, ts):
    bsz, seq, _ = proj3.shape
    return pl.pallas_call(
        _mem_attn_kernel,
        grid=(bsz, seq // ts),
        in_specs=[
            pl.BlockSpec((None, ts, C_WIDTH), lambda b, s: (b, s, COL_CQ)),
            pl.BlockSpec((None, ts, C_WIDTH), lambda b, s: (b, s, COL_CZ)),
            pl.BlockSpec((None, N_MEM, C_WIDTH), lambda b, s: (b, 0, 0)),
            pl.BlockSpec((None, N_MEM, C_WIDTH), lambda b, s: (b, 0, 1)),
            pl.BlockSpec((1, C_HEAD_DIM), lambda b, s: (0, 0)),
            pl.BlockSpec((1, C_HEAD_DIM), lambda b, s: (0, 0)),
        ],
        out_specs=pl.BlockSpec((None, ts, C_WIDTH), lambda b, s: (b, s, 0)),
        out_shape=jax.ShapeDtypeStruct((bsz, seq, C_WIDTH), BF16),
        compiler_params=pltpu.CompilerParams(
            dimension_semantics=("parallel", "parallel"), vmem_limit_bytes=VMEM_LIMIT),
        name="mem_attn",
    )(proj3, proj3, mkv3, mkv3, gq.reshape(1, -1), gk.reshape(1, -1))


def _shift_rows(x, carry_row):
    rolled = pltpu.roll(x, 1, 0)
    row = lax.broadcasted_iota(jnp.int32, x.shape, 0)
    return jnp.where(row == 0, carry_row, rolled)


def _seg_sum(x, seg_ones2):
    hi, lo = _split2(x)
    return jnp.dot(jnp.concatenate([hi, lo], axis=1), seg_ones2, preferred_element_type=F32)


def _drain(gen):
    for _ in gen:
        pass


def _rwkv_kernel(pr_ref, pk_ref, pv_ref, pz_ref, la_ref,
                 mur_ref, muk_ref, muv_ref, mula_ref, w0_ref, a0_ref, w2a2_ref,
                 kk_ref, ka_ref, rk_ref, lnw_ref, lnb_ref,
                 o_ref,
                 st_ref, cr_ref, ck_ref, cv_ref, cla_ref):
    L = CHUNK
    n_chunks = pr_ref.shape[0] // L
    n_pairs = B_HEADS // 2
    pair_lanes = [slice(p * LANES, (p + 1) * LANES) for p in range(n_pairs)]

    @pl.when(pl.program_id(1) == 0)
    def _():
        st_ref[...] = jnp.zeros_like(st_ref)
        cr_ref[...] = jnp.zeros_like(cr_ref)
        ck_ref[...] = jnp.zeros_like(ck_ref)
        cv_ref[...] = jnp.zeros_like(cv_ref)
        cla_ref[...] = jnp.zeros_like(cla_ref)

    lane = lax.broadcasted_iota(jnp.int32, (L, LANES), 1)
    lo_half = lane < B_HEAD_DIM
    eye_packed = ((lane & (B_HEAD_DIM - 1)) == lax.broadcasted_iota(jnp.int32, (L, LANES), 0)).astype(F32)
    lo_half2 = (lax.broadcasted_iota(jnp.int32, (L, 2 * LANES), 1) & B_HEAD_DIM) == 0
    ti = lax.broadcasted_iota(jnp.int32, (L, L), 0)
    si = lax.broadcasted_iota(jnp.int32, (L, L), 1)
    tri = (si <= ti).astype(BF16)
    tri3 = jnp.concatenate([tri, tri, tri], axis=1)
    ri = lax.broadcasted_iota(jnp.int32, (LANES, LANES), 0)
    ci = lax.broadcasted_iota(jnp.int32, (LANES, LANES), 1)
    same_head = (ri < B_HEAD_DIM) == (ci < B_HEAD_DIM)
    seg_ones = jnp.concatenate([same_head.astype(BF16)] * 2, axis=0)
    inv_n = 1.0 / B_HEAD_DIM
    mi = lax.broadcasted_iota(jnp.int32, (2 * L, 4 * L), 0)
    mj = lax.broadcasted_iota(jnp.int32, (2 * L, 4 * L), 1)
    tt, ss = mi & (L - 1), mj & (L - 1)
    tri_mask = (ss < tt) | ((mi >= L) & (ss == tt))
    w_hi, w_lo = _split2(w2a2_ref[...])
    w_hhl = jnp.concatenate([w_hi, w_hi, w_lo], axis=0)

    def stack_heads(x2):
        lo = lo_half if x2.shape[1] == LANES else lo_half2
        xb = x2.astype(BF16)
        return jnp.concatenate([jnp.where(lo, xb, 0.0), jnp.where(lo, 0.0, xb)], axis=0).astype(BF16)

    def dot3(x):
        x_hi, x_lo = _split2(x)
        return jnp.dot(jnp.concatenate([x_hi, x_lo, x_hi], axis=1), w_hhl, preferred_element_type=F32)

    prev_row = [cr_ref[0:1, :], ck_ref[0:1, :], cv_ref[0:1, :], cla_ref[0:1, :]]

    def prep(c, out):
        rows = slice(c * L, (c + 1) * L)

        def lerp(x_ref, idx, mu_ref):
            x = x_ref[rows, :]
            prev = _shift_rows(x, prev_row[idx])
            prev_row[idx] = x[L - 1:L, :]
            return x + mu_ref[...] * (prev - x)

        r = lerp(pr_ref, 0, mur_ref)
        k = lerp(pk_ref, 1, muk_ref)
        v = lerp(pv_ref, 2, muv_ref)
        la = lerp(la_ref, 3, mula_ref)
        xw = jnp.where(lo_half, jnp.tanh(la), 0.0)
        xa = jnp.where(lo_half, 0.0, la)
        zw = -(w0_ref[...] + dot3(xw))
        softplus = jnp.maximum(zw, 0.0) + jnp.log(1.0 + jnp.exp(-jnp.abs(zw)))
        lw = -jnp.exp(-softplus - 0.5)
        a_sig = jax.nn.sigmoid(a0_ref[...] + dot3(xa))
        yield
        cum = jnp.dot(tri3, jnp.concatenate(_split3(lw), axis=0), preferred_element_type=F32)
        e_pos = jnp.exp(cum)
        e_neg = jnp.exp(-cum)
        e_prev = jnp.exp(cum - lw)
        yield
        for p, sl in enumerate(pair_lanes):
            kk_raw = k[:, sl] * kk_ref[:, sl]
            kmod = k[:, sl] * (1.0 + (a_sig[:, sl] - 1.0) * ka_ref[:, sl])
            sums = _seg_sum(jnp.concatenate([kk_raw * kk_raw, r[:, sl] * kmod * rk_ref[:, sl]], axis=0), seg_ones)
            kk = kk_raw * lax.rsqrt(jnp.maximum(sums[:L], 1e-24))
            bt = (kk * a_sig[:, sl]) * e_neg[:, sl]
            kt = kmod * e_neg[:, sl]
            out.append(dict(
                at=(-kk) * e_prev[:, sl], rt=r[:, sl] * e_pos[:, sl], bt=bt, kt=kt,
                bks=jnp.concatenate([stack_heads(bt), stack_heads(kt)], axis=0),
                vb=v[:, sl].astype(BF16), vs=stack_heads(v[:, sl]),
                bonus=sums[L:] * v[:, sl], decay=e_pos[L - 1:L, sl]))
            if p % 2 == 1:
                yield

    def scores(ops):
        for d in ops:
            d["sm"] = jnp.where(tri_mask, _bdot_nt(jnp.concatenate([d["at"], d["rt"]], axis=0), d["bks"]),
                                0.0).astype(BF16)
        for d in ops:
            d["av"] = jnp.dot(d["sm"][:, 2 * L:], d["vs"], preferred_element_type=F32)
            d["x"] = jnp.concatenate([d["at"], d["av"][:L]], axis=1)
            d["pw"] = d["sm"][:L, :2 * L]

    def post(c, ops, o2):
        rows = slice(c * L, (c + 1) * L)
        for p, sl in enumerate(pair_lanes):
            mean = _seg_sum(o2[p], seg_ones) * inv_n
            dev = o2[p] - mean
            var = _seg_sum(dev * dev, seg_ones) * inv_n
            on = dev * lax.rsqrt(var + GN_EPS) * lnw_ref[:, sl] + lnb_ref[:, sl]
            o_ref[rows, sl] = ((on + ops[p]["bonus"]) * _silu(pz_ref[rows, sl])).astype(o_ref.dtype)
            if p % 2 == 1:
                yield

    st = [st_ref[p] for p in range(n_pairs)]
    ops = [[] for _ in range(n_chunks)]
    _drain(prep(0, ops[0]))
    scores(ops[0])
    side = []
    for c in range(n_chunks):
        cur = ops[c]
        if c + 1 < n_chunks:
            side.append(prep(c + 1, ops[c + 1]))
        def spread():
            for gen in side:
                next(gen, None)

        for d in cur:
            d["t"] = eye_packed + d["pw"].astype(F32)
            d["pw"] = jnp.dot(d["pw"], stack_heads(d["pw"]), preferred_element_type=F32).astype(BF16)
        spread()
        n = 2
        while 2 * n < L:
            for d in cur:
                both = jnp.dot(jnp.concatenate([d["pw"], d["t"].astype(BF16)], axis=0), stack_heads(d["pw"]),
                               preferred_element_type=F32)
                d["t"] = d["t"] + both[L:]
                d["pw"] = both[:L].astype(BF16)
            n *= 2
            spread()
        for d in cur:
            d["t"] = d["t"] + jnp.dot(d["t"].astype(BF16), stack_heads(d["pw"]), preferred_element_type=F32)
        spread()
        for d in cur:
            d["x"] = jnp.dot(d["t"].astype(BF16), stack_heads(d["x"]), preferred_element_type=F32)
        for gen in side:
            _drain(gen)
        side = []
        if c + 1 < n_chunks:
            scores(ops[c + 1])
        hs = [_bdot_nt(jnp.concatenate([d["x"][:, :LANES], d["rt"]], axis=0), st[p]) for p, d in enumerate(cur)]
        us = [hs[p][:L] + d["x"][:, LANES:] for p, d in enumerate(cur)]
        o2 = [hs[p][L:] + jnp.dot(d["sm"][L:, :2 * L], stack_heads(us[p]), preferred_element_type=F32)
              + d["av"][L:] for p, d in enumerate(cur)]
        for p, d in enumerate(cur):
            upd = _bdot_tn(jnp.concatenate([us[p].astype(BF16), d["vb"]], axis=0),
                           jnp.concatenate([d["bt"], d["kt"]], axis=0))
            st[p] = (st[p] + jnp.where(same_head, upd, 0.0)) * d["decay"]
        side.append(post(c, cur, o2))
    for gen in side:
        _drain(gen)

    for p in range(n_pairs):
        st_ref[p] = st[p]
    cr_ref[0:1, :], ck_ref[0:1, :], cv_ref[0:1, :], cla_ref[0:1, :] = prev_row


def _rwkv(proj3, lora3, mu_rkv, mu_la, w0, a0, w2a2, k_k, k_a, r_k, ln_w, ln_b, tokens_per_step):
    bsz, seq, _ = proj3.shape
    tb = tokens_per_step

    def col(group):
        return pl.BlockSpec((None, tb, B_WIDTH), lambda b, c: (b, c, group))

    def vec(n):
        return pl.BlockSpec((1, n), lambda b, c: (0, 0))

    row = lambda t: t.reshape(1, -1)
    return pl.pallas_call(
        _rwkv_kernel,
        grid=(bsz, seq // tb),
        in_specs=[
            col(COL_BR), col(COL_BK), col(COL_BV), col(COL_BZ),
            pl.BlockSpec((None, tb, 2 * LORA), lambda b, c: (b, c, 0)),
            vec(B_WIDTH), vec(B_WIDTH), vec(B_WIDTH), vec(2 * LORA), vec(B_WIDTH), vec(B_WIDTH),
            pl.BlockSpec((2 * LORA, B_WIDTH), lambda b, c: (0, 0)),
            vec(B_WIDTH), vec(B_WIDTH), vec(B_WIDTH), vec(B_WIDTH), vec(B_WIDTH),
        ],
        out_specs=pl.BlockSpec((None, tb, B_WIDTH), lambda b, c: (b, c, 0)),
        out_shape=jax.ShapeDtypeStruct((bsz, seq, B_WIDTH), BF16),
        scratch_shapes=[
            pltpu.VMEM((B_HEADS // 2, LANES, LANES), F32),
            pltpu.VMEM((8, B_WIDTH), F32), pltpu.VMEM((8, B_WIDTH), F32), pltpu.VMEM((8, B_WIDTH), F32),
            pltpu.VMEM((8, 2 * LORA), F32),
        ],
        compiler_params=pltpu.CompilerParams(
            dimension_semantics=("parallel", "arbitrary"), vmem_limit_bytes=VMEM_LIMIT),
        name="rwkv",
    )(proj3, proj3, proj3, proj3, lora3,
      row(mu_rkv[0]), row(mu_rkv[1]), row(mu_rkv[2]), row(mu_la), row(w0), row(a0), w2a2,
      row(k_k), row(k_a), row(r_k), row(ln_w), row(ln_b))


def _merge_out_kernel(ya_ref, yb_ref, yc_ref, ga_ref, gb_ref, gc_ref, wa_ref, wb_ref, wc_ref, wo_ref, x_ref,
                      o_ref):
    merged = (jax.nn.sigmoid(ga_ref[...]) * jnp.dot(ya_ref[...], wa_ref[...], preferred_element_type=F32)
              + jax.nn.sigmoid(gb_ref[...]) * jnp.dot(yb_ref[...], wb_ref[...], preferred_element_type=F32)
              + jax.nn.sigmoid(gc_ref[...]) * jnp.dot(yc_ref[...], wc_ref[...], preferred_element_type=F32))
    o_ref[...] = x_ref[...] + jnp.dot(merged.astype(BF16), wo_ref[...], preferred_element_type=F32)


def _merge_out(ya, yb, yc, proj2, wa, wb, wc, wo, x2d, tm):
    m = x2d.shape[0]

    def ybranch(width):
        return pl.BlockSpec((tm, width), lambda i: (i, 0))

    def gate(group):
        return pl.BlockSpec((tm, D_MODEL), lambda i: (i, group * 1024 // D_MODEL))

    def resident(shape):
        return pl.BlockSpec(shape, lambda i: (0, 0), pipeline_mode=pl.Buffered(1))

    return pl.pallas_call(
        _merge_out_kernel,
        grid=(m // tm,),
        in_specs=[
            ybranch(A_WIDTH), ybranch(B_WIDTH), ybranch(C_WIDTH),
            gate(COL_GA), gate(COL_GB), gate(COL_GC),
            resident((A_WIDTH, D_MODEL)), resident((B_WIDTH, D_MODEL)), resident((C_WIDTH, D_MODEL)),
            resident((D_MODEL, D_MODEL)),
            pl.BlockSpec((tm, D_MODEL), lambda i: (i, 0)),
        ],
        out_specs=pl.BlockSpec((tm, D_MODEL), lambda i: (i, 0)),
        out_shape=jax.ShapeDtypeStruct((m, D_MODEL), F32),
        compiler_params=pltpu.CompilerParams(
            dimension_semantics=("parallel",), vmem_limit_bytes=VMEM_LIMIT),
        name="merge_out",
    )(ya, yb, yc, proj2, proj2, proj2, wa, wb, wc, wo, x2d)


def _layer(x, mem, norm_g, w_in, a_q_g, a_k_g, a_rel_bias, w_up_a,
           b_mu_rkv, b_mu_w, b_mu_a, b_w0, b_w2, b_a0, b_a2, b_k_k, b_k_a, b_r_k,
           b_ln_w, b_ln_b, w_up_b, mem_norm_g, w_mem_kv, c_q_g, c_k_g, w_up_c, w_o):
    bsz, seq, d = x.shape
    t = bsz * seq
    x2d = x.reshape(t, d)
    w_main, w_lora = _prep_w_in(w_in, 128)
    wa, wb, wc, wo, wm = _cast_weights([w_up_a, w_up_b, w_up_c, w_o, w_mem_kv], 8)

    proj, lora = _in_proj(x2d, norm_g, w_main, w_lora, 1024, 1024)
    mkv = _norm_matmul(mem.reshape(bsz * N_MEM, d), mem_norm_g, wm, 1024, 1024, "mem_kv")

    proj3 = proj.reshape(bsz, seq, MAIN_COLS)
    ya = _band_attn(proj3, _band_bias_table(a_rel_bias), a_q_g, a_k_g)
    yc = _mem_attn(proj3, mkv.reshape(bsz, N_MEM, 2 * C_WIDTH), c_q_g, c_k_g, 512)
    yb = _rwkv(proj3, lora.reshape(bsz, seq, 2 * LORA), b_mu_rkv,
               jnp.concatenate([b_mu_w, b_mu_a]), b_w0, b_a0,
               jnp.concatenate([b_w2, b_a2], axis=0), b_k_k, b_k_a, b_r_k.reshape(-1), b_ln_w, b_ln_b, 8 * CHUNK)

    out = _merge_out(ya.reshape(t, A_WIDTH), yb.reshape(t, B_WIDTH), yc.reshape(t, C_WIDTH), proj,
                     wa, wb, wc, wo, x2d, 256)
    return out.reshape(bsz, seq, d)


def kernel(x, mem, norm_g, w_in, a_q_g, a_k_g, a_rel_bias, w_up_a, b_mu_rkv, b_mu_w, b_mu_a, b_w0, b_w2, b_a0, b_a2, b_k_k, b_k_a, b_r_k, b_ln_w, b_ln_b, w_up_b, mem_norm_g, w_mem_kv, c_q_g, c_k_g, w_up_c, w_o):
    for l in range(norm_g.shape[0]):
        x = _layer(x, mem, norm_g[l], w_in[l], a_q_g[l], a_k_g[l], a_rel_bias[l], w_up_a[l],
                   b_mu_rkv[l], b_mu_w[l], b_mu_a[l], b_w0[l], b_w2[l], b_a0[l], b_a2[l],
                   b_k_k[l], b_k_a[l], b_r_k[l], b_ln_w[l], b_ln_b[l], w_up_b[l],
                   mem_norm_g[l], w_mem_kv[l], c_q_g[l], c_k_g[l], w_up_c[l], w_o[l])
    return x
```

```python
import numpy as np
import jax
import jax.numpy as jnp
from jax import lax
from jax.experimental import pallas as pl
from jax.experimental.pallas import tpu as pltpu

D_MODEL = 2048
CHUNK = 64
N_MEM = 256
NORM_EPS = 1e-6
NEG_INF = -1e30

A_HEADS = 8
A_HEAD_DIM = 128
A_WIDTH = A_HEADS * A_HEAD_DIM
A_LEFT_CHUNKS = 8
REL_CLIP = 128

B_HEADS = 16
B_HEAD_DIM = 64
B_WIDTH = B_HEADS * B_HEAD_DIM
LORA = 64
GN_EPS = 64e-5

C_HEADS = 4
C_HEAD_DIM = 256
C_WIDTH = C_HEADS * C_HEAD_DIM

LANES = 128
VMEM_LIMIT = 56 * 1024 * 1024

A_BAND = (A_LEFT_CHUNKS + 1) * CHUNK
K_WINDOW = -(-A_BAND // LANES) * LANES
K_SLACK = K_WINDOW - A_BAND
K_PAD = K_WINDOW - CHUNK
ATTN_CHUNKS_IN_FLIGHT = 16
NORM_SUB_ROWS = 256

COL_AQ, COL_AK, COL_AV, COL_AZ = 0, 1, 2, 3
COL_BR, COL_BK, COL_BV, COL_BZ = 4, 5, 6, 7
COL_CQ, COL_CZ = 8, 9
COL_GA, COL_GB, COL_GC = 10, 12, 14
MAIN_COLS = 16 * 1024
LORA_LO = 4 * A_WIDTH + 4 * B_WIDTH
LORA_HI = LORA_LO + 2 * LORA

BF16 = jnp.bfloat16
F32 = jnp.float32


def _bdot(a, b):
    return jnp.dot(a.astype(BF16), b.astype(BF16), preferred_element_type=F32)


def _bdot_nt(a, b):
    return lax.dot_general(a.astype(BF16), b.astype(BF16), (((1,), (1,)), ((), ())),
                           preferred_element_type=F32)


def _bdot_tn(a, b):
    return lax.dot_general(a.astype(BF16), b.astype(BF16), (((0,), (0,)), ((), ())),
                           preferred_element_type=F32)


def _split2(x):
    hi = x.astype(BF16)
    lo = (x - hi.astype(F32)).astype(BF16)
    return hi, lo


def _split3(x):
    hi = x.astype(BF16)
    r1 = x - hi.astype(F32)
    mid = r1.astype(BF16)
    lo = (r1 - mid.astype(F32)).astype(BF16)
    return hi, mid, lo


def _silu(x):
    return x * jax.nn.sigmoid(x)


def _prep_w_in_kernel(w_ref, main_ref, lora_ref):
    main_ref[:, :LORA_LO] = w_ref[:, :LORA_LO].astype(BF16)
    main_ref[:, LORA_LO:] = w_ref[:, LORA_HI:].astype(BF16)
    lora_ref[...] = w_ref[:, LORA_LO:LORA_HI].astype(BF16)


def _prep_w_in(w_in, rows):
    k, n = w_in.shape
    return pl.pallas_call(
        _prep_w_in_kernel,
        grid=(k // rows,),
        in_specs=[pl.BlockSpec((rows, n), lambda i: (i, 0))],
        out_specs=[
            pl.BlockSpec((rows, MAIN_COLS), lambda i: (i, 0)),
            pl.BlockSpec((rows, 2 * LORA), lambda i: (i, 0)),
        ],
        out_shape=[
            jax.ShapeDtypeStruct((k, MAIN_COLS), BF16),
            jax.ShapeDtypeStruct((k, 2 * LORA), BF16),
        ],
        compiler_params=pltpu.CompilerParams(
            dimension_semantics=("parallel",), vmem_limit_bytes=VMEM_LIMIT),
        name="prep_w_in",
    )(w_in)


def _cast_weights_kernel(*refs):
    n = len(refs) // 2
    for src, dst in zip(refs[:n], refs[n:]):
        dst[...] = src[...].astype(BF16)


def _cast_weights(ws, n_steps):
    specs = [pl.BlockSpec((w.shape[0] // n_steps, w.shape[1]), lambda i: (i, 0)) for w in ws]
    return pl.pallas_call(
        _cast_weights_kernel,
        grid=(n_steps,),
        in_specs=specs,
        out_specs=specs,
        out_shape=[jax.ShapeDtypeStruct(w.shape, BF16) for w in ws],
        compiler_params=pltpu.CompilerParams(
            dimension_semantics=("parallel",), vmem_limit_bytes=VMEM_LIMIT),
        name="cast_weights",
    )(*ws)


def _norm_matmul_kernel(x_ref, g_ref, w_ref, o_ref, h_ref):
    @pl.when(pl.program_id(1) == 0)
    def _():
        x = x_ref[...]
        ms = jnp.mean(x * x, axis=-1, keepdims=True)
        h_ref[...] = (x * lax.rsqrt(ms + NORM_EPS) * g_ref[...]).astype(BF16)

    o_ref[...] = jnp.dot(h_ref[...], w_ref[...], preferred_element_type=F32).astype(o_ref.dtype)


def _norm_matmul(x2d, g, w_bf16, tm, tn, name):
    m, k = x2d.shape
    n = w_bf16.shape[1]
    return pl.pallas_call(
        _norm_matmul_kernel,
        grid=(m // tm, n // tn),
        in_specs=[
            pl.BlockSpec((tm, k), lambda i, j: (i, 0)),
            pl.BlockSpec((1, k), lambda i, j: (0, 0)),
            pl.BlockSpec((k, tn), lambda i, j: (0, j)),
        ],
        out_specs=pl.BlockSpec((tm, tn), lambda i, j: (i, j)),
        out_shape=jax.ShapeDtypeStruct((m, n), F32),
        scratch_shapes=[pltpu.VMEM((tm, k), BF16)],
        compiler_params=pltpu.CompilerParams(
            dimension_semantics=("parallel", "arbitrary"), vmem_limit_bytes=VMEM_LIMIT),
        name=name,
    )(x2d, g.reshape(1, k), w_bf16)


def _in_proj_kernel(x_ref, g_ref, w_ref, wla_ref, o_ref, la_ref, h_ref):
    j = pl.program_id(1)

    @pl.when(j == 0)
    def _():
        for r0 in range(0, x_ref.shape[0], NORM_SUB_ROWS):
            rows = slice(r0, r0 + NORM_SUB_ROWS)
            x = x_ref[rows, :]
            ms = jnp.mean(x * x, axis=-1, keepdims=True)
            h = (x * lax.rsqrt(ms + NORM_EPS) * g_ref[...]).astype(BF16)
            h_ref[rows, :] = h
            la_ref[rows, :] = jnp.dot(h, wla_ref[...], preferred_element_type=F32)
            o_ref[rows, :] = jnp.dot(h, w_ref[...], preferred_element_type=F32)

    @pl.when(j != 0)
    def _():
        o_ref[...] = jnp.dot(h_ref[...], w_ref[...], preferred_element_type=F32)


def _in_proj(x2d, g, w_main, w_lora, tm, tn):
    m, k = x2d.shape
    n = w_main.shape[1]
    return pl.pallas_call(
        _in_proj_kernel,
        grid=(m // tm, n // tn),
        in_specs=[
            pl.BlockSpec((tm, k), lambda i, j: (i, 0)),
            pl.BlockSpec((1, k), lambda i, j: (0, 0)),
            pl.BlockSpec((k, tn), lambda i, j: (0, j)),
            pl.BlockSpec((k, 2 * LORA), lambda i, j: (0, 0)),
        ],
        out_specs=[
            pl.BlockSpec((tm, tn), lambda i, j: (i, j)),
            pl.BlockSpec((tm, 2 * LORA), lambda i, j: (i, 0)),
        ],
        out_shape=[
            jax.ShapeDtypeStruct((m, n), F32),
            jax.ShapeDtypeStruct((m, 2 * LORA), F32),
        ],
        scratch_shapes=[pltpu.VMEM((tm, k), BF16)],
        compiler_params=pltpu.CompilerParams(
            dimension_semantics=("parallel", "arbitrary"), vmem_limit_bytes=VMEM_LIMIT),
        name="in_proj",
    )(x2d, g.reshape(1, k), w_main, w_lora)


def _band_attn_kernel(q_ref, k_ref, v_ref, z_ref, bias_ref, gq_ref, gk_ref, o_ref,
                      qn_ref, kp_ref, vp_ref):
    seq = q_ref.shape[0]
    scale = A_HEAD_DIM ** -0.5
    q = q_ref[...]
    qn = q * lax.rsqrt(jnp.mean(q * q, axis=-1, keepdims=True) + NORM_EPS) * (gq_ref[...] * scale)
    qn_ref[...] = qn.astype(BF16)
    k = k_ref[...]
    kn = k * lax.rsqrt(jnp.mean(k * k, axis=-1, keepdims=True) + NORM_EPS) * gk_ref[...]
    kp_ref[0:K_PAD, :] = jnp.zeros((K_PAD, A_HEAD_DIM), BF16)
    vp_ref[0:K_PAD, :] = jnp.zeros((K_PAD, A_HEAD_DIM), BF16)
    kp_ref[K_PAD:, :] = kn.astype(BF16)
    vp_ref[K_PAD:, :] = v_ref[...].astype(BF16)
    def scores(c):
        q0 = c * CHUNK
        first = K_PAD - q0
        skip = max(first, 0) // LANES * LANES
        s = _bdot_nt(qn_ref[q0:q0 + CHUNK, :], kp_ref[q0 + skip:q0 + K_WINDOW, :]) + bias_ref[:, skip:]
        if first > skip:
            col = lax.broadcasted_iota(jnp.int32, s.shape, 1)
            s = jnp.where(col >= first - skip, s, NEG_INF)
        return s, skip

    for c0 in range(0, seq // CHUNK, ATTN_CHUNKS_IN_FLIGHT):
        group = range(c0, c0 + ATTN_CHUNKS_IN_FLIGHT)
        sc = [scores(c) for c in group]
        mx = [jnp.max(s, axis=-1, keepdims=True) for s, _ in sc]
        ps = [jnp.exp(s - m) for (s, _), m in zip(sc, mx)]
        ls = [jnp.sum(p, axis=-1, keepdims=True) for p in ps]
        outs = [_bdot(p, vp_ref[c * CHUNK + skip:c * CHUNK + K_WINDOW, :]) / l
                for c, p, l, (_, skip) in zip(group, ps, ls, sc)]
        for c, o in zip(group, outs):
            q0 = c * CHUNK
            o_ref[q0:q0 + CHUNK, :] = (o * _silu(z_ref[q0:q0 + CHUNK, :])).astype(o_ref.dtype)


def _band_bias_table(rel_bias):
    period = 1024
    q = np.arange(period)
    m_minus_r = np.where(q < K_WINDOW, q, q - period)
    dist = K_PAD - m_minus_r
    f = rel_bias[:, np.clip(dist, -REL_CLIP, REL_CLIP) + REL_CLIP].astype(F32)
    seq = jnp.tile(f, (1, CHUNK))[:, :CHUNK * (period - 1)]
    toeplitz = seq.reshape(-1, CHUNK, period - 1)[:, :, :K_WINDOW]
    in_band = np.arange(K_WINDOW)[None, None, :] >= K_SLACK
    return jnp.where(jnp.asarray(in_band), toeplitz, NEG_INF)


def _band_attn(proj3, bias_tab, gq, gk):
    bsz, seq, _ = proj3.shape
    hb = A_WIDTH // A_HEAD_DIM

    def col(group):
        return pl.BlockSpec((None, seq, A_HEAD_DIM), lambda b, h: (b, 0, group * hb + h))

    return pl.pallas_call(
        _band_attn_kernel,
        grid=(bsz, A_HEADS),
        in_specs=[
            col(COL_AQ), col(COL_AK), col(COL_AV), col(COL_AZ),
            pl.BlockSpec((None, CHUNK, K_WINDOW), lambda b, h: (h, 0, 0)),
            pl.BlockSpec((1, A_HEAD_DIM), lambda b, h: (0, 0)),
            pl.BlockSpec((1, A_HEAD_DIM), lambda b, h: (0, 0)),
        ],
        out_specs=pl.BlockSpec((None, seq, A_HEAD_DIM), lambda b, h: (b, 0, h)),
        out_shape=jax.ShapeDtypeStruct((bsz, seq, A_WIDTH), BF16),
        scratch_shapes=[
            pltpu.VMEM((seq, A_HEAD_DIM), BF16),
            pltpu.VMEM((seq + K_PAD, A_HEAD_DIM), BF16),
            pltpu.VMEM((seq + K_PAD, A_HEAD_DIM), BF16),
        ],
        compiler_params=pltpu.CompilerParams(
            dimension_semantics=("parallel", "parallel"), vmem_limit_bytes=VMEM_LIMIT),
        name="band_attn",
    )(proj3, proj3, proj3, proj3, bias_tab, gq.reshape(1, -1), gk.reshape(1, -1))


def _mem_attn_kernel(q_ref, z_ref, mk_ref, mv_ref, gq_ref, gk_ref, o_ref, kn_ref):
    scale = C_HEAD_DIM ** -0.5
    heads = [slice(h * C_HEAD_DIM, (h + 1) * C_HEAD_DIM) for h in range(C_HEADS)]

    def rms(x, g):
        return x * lax.rsqrt(jnp.mean(x * x, axis=-1, keepdims=True) + NORM_EPS) * g

    @pl.when(pl.program_id(1) == 0)
    def _():
        for sl in heads:
            kn_ref[:, sl] = rms(mk_ref[:, sl], gk_ref[...]).astype(BF16)

    qn = [rms(q_ref[:, sl], gq_ref[...] * scale) for sl in heads]
    kn = [kn_ref[:, sl] for sl in heads]
    sc = [_bdot_nt(q, k) for q, k in zip(qn, kn)]
    mx = [jnp.max(s, axis=-1, keepdims=True) for s in sc]
    ps = [jnp.exp(s - m) for s, m in zip(sc, mx)]
    ls = [jnp.sum(p, axis=-1, keepdims=True) for p in ps]
    outs = [_bdot(p, mv_ref[:, sl]) / l for p, l, sl in zip(ps, ls, heads)]
    for o, sl in zip(outs, heads):
        o_ref[:, sl] = (o * _silu(z_ref[:, sl])).astype(o_ref.dtype)


def _mem_attn(proj3, mkv3, gq, gk, ts):
    bsz, seq, _ = proj3.shape
    return pl.pallas_call(
        _mem_attn_kernel,
        grid=(bsz, seq // ts),
        in_specs=[
            pl.BlockSpec((None, ts, C_WIDTH), lambda b, s: (b, s, COL_CQ)),
            pl.BlockSpec((None, ts, C_WIDTH), lambda b, s: (b, s, COL_CZ)),
            pl.BlockSpec((None, N_MEM, C_WIDTH), lambda b, s: (b, 0, 0)),
            pl.BlockSpec((None, N_MEM, C_WIDTH), lambda b, s: (b, 0, 1)),
            pl.BlockSpec((1, C_HEAD_DIM), lambda b, s: (0, 0)),
            pl.BlockSpec((1, C_HEAD_DIM), lambda b, s: (0, 0)),
        ],
        out_specs=pl.BlockSpec((None, ts, C_WIDTH), lambda b, s: (b, s, 0)),
        out_shape=jax.ShapeDtypeStruct((bsz, seq, C_WIDTH), BF16),
        scratch_shapes=[pltpu.VMEM((N_MEM, C_WIDTH), BF16)],
        compiler_params=pltpu.CompilerParams(
            dimension_semantics=("parallel", "arbitrary"), vmem_limit_bytes=VMEM_LIMIT),
        name="mem_attn",
    )(proj3, proj3, mkv3, mkv3, gq.reshape(1, -1), gk.reshape(1, -1))


def _shift_rows(x, carry_row):
    rolled = pltpu.roll(x, 1, 0)
    row = lax.broadcasted_iota(jnp.int32, x.shape, 0)
    return jnp.where(row == 0, carry_row, rolled)


def _seg_sum(x, seg_ones2):
    hi, lo = _split2(x)
    return jnp.dot(jnp.concatenate([hi, lo], axis=1), seg_ones2, preferred_element_type=F32)


def _drain(gen):
    for _ in gen:
        pass


def _rwkv_kernel(pr_ref, pk_ref, pv_ref, pz_ref, la_ref,
                 mur_ref, muk_ref, muv_ref, mula_ref, w0_ref, a0_ref, w2a2_ref,
                 kk_ref, ka_ref, rk_ref, lnw_ref, lnb_ref,
                 o_ref,
                 st_ref, cr_ref, ck_ref, cv_ref, cla_ref):
    L = CHUNK
    n_chunks = pr_ref.shape[0] // L
    n_pairs = B_HEADS // 2
    pair_lanes = [slice(p * LANES, (p + 1) * LANES) for p in range(n_pairs)]

    @pl.when(pl.program_id(1) == 0)
    def _():
        st_ref[...] = jnp.zeros_like(st_ref)
        cr_ref[...] = jnp.zeros_like(cr_ref)
        ck_ref[...] = jnp.zeros_like(ck_ref)
        cv_ref[...] = jnp.zeros_like(cv_ref)
        cla_ref[...] = jnp.zeros_like(cla_ref)

    lane = lax.broadcasted_iota(jnp.int32, (L, LANES), 1)
    lo_half = lane < B_HEAD_DIM
    eye_packed = ((lane & (B_HEAD_DIM - 1)) == lax.broadcasted_iota(jnp.int32, (L, LANES), 0)).astype(F32)
    lo_half2 = (lax.broadcasted_iota(jnp.int32, (L, 2 * LANES), 1) & B_HEAD_DIM) == 0
    ti = lax.broadcasted_iota(jnp.int32, (L, L), 0)
    si = lax.broadcasted_iota(jnp.int32, (L, L), 1)
    tri = (si <= ti).astype(BF16)
    tri3 = jnp.concatenate([tri, tri, tri], axis=1)
    ri = lax.broadcasted_iota(jnp.int32, (LANES, LANES), 0)
    ci = lax.broadcasted_iota(jnp.int32, (LANES, LANES), 1)
    same_head = (ri < B_HEAD_DIM) == (ci < B_HEAD_DIM)
    seg_ones = jnp.concatenate([same_head.astype(BF16)] * 2, axis=0)
    inv_n = 1.0 / B_HEAD_DIM
    mi = lax.broadcasted_iota(jnp.int32, (2 * L, 4 * L), 0)
    mj = lax.broadcasted_iota(jnp.int32, (2 * L, 4 * L), 1)
    tt, ss = mi & (L - 1), mj & (L - 1)
    tri_mask = (ss < tt) | ((mi >= L) & (ss == tt))
    w_hi, w_lo = _split2(w2a2_ref[...])
    w_hhl = jnp.concatenate([w_hi, w_hi, w_lo], axis=0)

    def stack_heads(x2):
        lo = lo_half if x2.shape[1] == LANES else lo_half2
        xb = x2.astype(BF16)
        return jnp.concatenate([jnp.where(lo, xb, 0.0), jnp.where(lo, 0.0, xb)], axis=0).astype(BF16)

    def dot3(x):
        x_hi, x_lo = _split2(x)
        return jnp.dot(jnp.concatenate([x_hi, x_lo, x_hi], axis=1), w_hhl, preferred_element_type=F32)

    prev_row = [cr_ref[0:1, :], ck_ref[0:1, :], cv_ref[0:1, :], cla_ref[0:1, :]]

    def prep(c, out):
        rows = slice(c * L, (c + 1) * L)

        def lerp(x_ref, idx, mu_ref):
            x = x_ref[rows, :]
            prev = _shift_rows(x, prev_row[idx])
            prev_row[idx] = x[L - 1:L, :]
            return x + mu_ref[...] * (prev - x)

        r = lerp(pr_ref, 0, mur_ref)
        k = lerp(pk_ref, 1, muk_ref)
        v = lerp(pv_ref, 2, muv_ref)
        la = lerp(la_ref, 3, mula_ref)
        xw = jnp.where(lo_half, jnp.tanh(la), 0.0)
        xa = jnp.where(lo_half, 0.0, la)
        zw = -(w0_ref[...] + dot3(xw))
        softplus = jnp.maximum(zw, 0.0) + jnp.log(1.0 + jnp.exp(-jnp.abs(zw)))
        lw = -jnp.exp(-softplus - 0.5)
        a_sig = jax.nn.sigmoid(a0_ref[...] + dot3(xa))
        yield
        cum = jnp.dot(tri3, jnp.concatenate(_split3(lw), axis=0), preferred_element_type=F32)
        e_pos = jnp.exp(cum)
        e_neg = jnp.exp(-cum)
        e_prev = jnp.exp(cum - lw)
        yield
        for p, sl in enumerate(pair_lanes):
            kk_raw = k[:, sl] * kk_ref[:, sl]
            kmod = k[:, sl] * (1.0 + (a_sig[:, sl] - 1.0) * ka_ref[:, sl])
            sums = _seg_sum(jnp.concatenate([kk_raw * kk_raw, r[:, sl] * kmod * rk_ref[:, sl]], axis=0), seg_ones)
            kk = kk_raw * lax.rsqrt(jnp.maximum(sums[:L], 1e-24))
            bt = (kk * a_sig[:, sl]) * e_neg[:, sl]
            kt = kmod * e_neg[:, sl]
            out.append(dict(
                at=(-kk) * e_prev[:, sl], rt=r[:, sl] * e_pos[:, sl], bt=bt, kt=kt,
                bks=jnp.concatenate([stack_heads(bt), stack_heads(kt)], axis=0),
                vb=v[:, sl].astype(BF16), vs=stack_heads(v[:, sl]),
                bonus=sums[L:] * v[:, sl], decay=e_pos[L - 1:L, sl]))
            if p % 2 == 1:
                yield

    def scores(ops):
        for d in ops:
            d["sm"] = jnp.where(tri_mask, _bdot_nt(jnp.concatenate([d["at"], d["rt"]], axis=0), d["bks"]),
                                0.0).astype(BF16)
        for d in ops:
            d["av"] = jnp.dot(d["sm"][:, 2 * L:], d["vs"], preferred_element_type=F32)
            d["x"] = jnp.concatenate([d["at"], d["av"][:L]], axis=1)
            d["pw"] = d["sm"][:L, :2 * L]

    def post(c, ops, o2):
        rows = slice(c * L, (c + 1) * L)
        for p, sl in enumerate(pair_lanes):
            mean = _seg_sum(o2[p], seg_ones) * inv_n
            dev = o2[p] - mean
            var = _seg_sum(dev * dev, seg_ones) * inv_n
            on = dev * lax.rsqrt(var + GN_EPS) * lnw_ref[:, sl] + lnb_ref[:, sl]
            o_ref[rows, sl] = ((on + ops[p]["bonus"]) * _silu(pz_ref[rows, sl])).astype(o_ref.dtype)
            if p % 2 == 1:
                yield

    st = [st_ref[p] for p in range(n_pairs)]
    ops = [[] for _ in range(n_chunks)]
    _drain(prep(0, ops[0]))
    scores(ops[0])
    side = []
    for c in range(n_chunks):
        cur = ops[c]
        if c + 1 < n_chunks:
            side.append(prep(c + 1, ops[c + 1]))
        def spread():
            for gen in side:
                next(gen, None)

        for d in cur:
            d["t"] = eye_packed + d["pw"].astype(F32)
            d["pw"] = jnp.dot(d["pw"], stack_heads(d["pw"]), preferred_element_type=F32).astype(BF16)
        spread()
        n = 2
        while 2 * n < L:
            for d in cur:
                both = jnp.dot(jnp.concatenate([d["pw"], d["t"].astype(BF16)], axis=0), stack_heads(d["pw"]),
                               preferred_element_type=F32)
                d["t"] = d["t"] + both[L:]
                d["pw"] = both[:L].astype(BF16)
            n *= 2
            spread()
        for d in cur:
            d["t"] = d["t"] + jnp.dot(d["t"].astype(BF16), stack_heads(d["pw"]), preferred_element_type=F32)
        spread()
        for d in cur:
            d["x"] = jnp.dot(d["t"].astype(BF16), stack_heads(d["x"]), preferred_element_type=F32)
        for gen in side:
            _drain(gen)
        side = []
        if c + 1 < n_chunks:
            scores(ops[c + 1])
        hs = [_bdot_nt(jnp.concatenate([d["x"][:, :LANES], d["rt"]], axis=0), st[p]) for p, d in enumerate(cur)]
        us = [hs[p][:L] + d["x"][:, LANES:] for p, d in enumerate(cur)]
        o2 = [hs[p][L:] + jnp.dot(d["sm"][L:, :2 * L], stack_heads(us[p]), preferred_element_type=F32)
              + d["av"][L:] for p, d in enumerate(cur)]
        for p, d in enumerate(cur):
            upd = _bdot_tn(jnp.concatenate([us[p].astype(BF16), d["vb"]], axis=0),
                           jnp.concatenate([d["bt"], d["kt"]], axis=0))
            st[p] = (st[p] + jnp.where(same_head, upd, 0.0)) * d["decay"]
        side.append(post(c, cur, o2))
    for gen in side:
        _drain(gen)

    for p in range(n_pairs):
        st_ref[p] = st[p]
    cr_ref[0:1, :], ck_ref[0:1, :], cv_ref[0:1, :], cla_ref[0:1, :] = prev_row


def _rwkv(proj3, lora3, mu_rkv, mu_la, w0, a0, w2a2, k_k, k_a, r_k, ln_w, ln_b, tokens_per_step):
    bsz, seq, _ = proj3.shape
    tb = tokens_per_step

    def col(group):
        return pl.BlockSpec((None, tb, B_WIDTH), lambda b, c: (b, c, group))

    def vec(n):
        return pl.BlockSpec((1, n), lambda b, c: (0, 0))

    row = lambda t: t.reshape(1, -1)
    return pl.pallas_call(
        _rwkv_kernel,
        grid=(bsz, seq // tb),
        in_specs=[
            col(COL_BR), col(COL_BK), col(COL_BV), col(COL_BZ),
            pl.BlockSpec((None, tb, 2 * LORA), lambda b, c: (b, c, 0)),
            vec(B_WIDTH), vec(B_WIDTH), vec(B_WIDTH), vec(2 * LORA), vec(B_WIDTH), vec(B_WIDTH),
            pl.BlockSpec((2 * LORA, B_WIDTH), lambda b, c: (0, 0)),
            vec(B_WIDTH), vec(B_WIDTH), vec(B_WIDTH), vec(B_WIDTH), vec(B_WIDTH),
        ],
        out_specs=pl.BlockSpec((None, tb, B_WIDTH), lambda b, c: (b, c, 0)),
        out_shape=jax.ShapeDtypeStruct((bsz, seq, B_WIDTH), BF16),
        scratch_shapes=[
            pltpu.VMEM((B_HEADS // 2, LANES, LANES), F32),
            pltpu.VMEM((8, B_WIDTH), F32), pltpu.VMEM((8, B_WIDTH), F32), pltpu.VMEM((8, B_WIDTH), F32),
            pltpu.VMEM((8, 2 * LORA), F32),
        ],
        compiler_params=pltpu.CompilerParams(
            dimension_semantics=("parallel", "arbitrary"), vmem_limit_bytes=VMEM_LIMIT),
        name="rwkv",
    )(proj3, proj3, proj3, proj3, lora3,
      row(mu_rkv[0]), row(mu_rkv[1]), row(mu_rkv[2]), row(mu_la), row(w0), row(a0), w2a2,
      row(k_k), row(k_a), row(r_k), row(ln_w), row(ln_b))


def _merge_out_kernel(ya_ref, yb_ref, yc_ref, ga_ref, gb_ref, gc_ref, wa_ref, wb_ref, wc_ref, wo_ref, x_ref,
                      o_ref):
    merged = (jax.nn.sigmoid(ga_ref[...]) * jnp.dot(ya_ref[...], wa_ref[...], preferred_element_type=F32)
              + jax.nn.sigmoid(gb_ref[...]) * jnp.dot(yb_ref[...], wb_ref[...], preferred_element_type=F32)
              + jax.nn.sigmoid(gc_ref[...]) * jnp.dot(yc_ref[...], wc_ref[...], preferred_element_type=F32))
    o_ref[...] = x_ref[...] + jnp.dot(merged.astype(BF16), wo_ref[...], preferred_element_type=F32)


def _merge_out(ya, yb, yc, proj2, wa, wb, wc, wo, x2d, tm):
    m = x2d.shape[0]

    def ybranch(width):
        return pl.BlockSpec((tm, width), lambda i: (i, 0))

    def gate(group):
        return pl.BlockSpec((tm, D_MODEL), lambda i: (i, group * 1024 // D_MODEL))

    def resident(shape):
        return pl.BlockSpec(shape, lambda i: (0, 0), pipeline_mode=pl.Buffered(1))

    return pl.pallas_call(
        _merge_out_kernel,
        grid=(m // tm,),
        in_specs=[
            ybranch(A_WIDTH), ybranch(B_WIDTH), ybranch(C_WIDTH),
            gate(COL_GA), gate(COL_GB), gate(COL_GC),
            resident((A_WIDTH, D_MODEL)), resident((B_WIDTH, D_MODEL)), resident((C_WIDTH, D_MODEL)),
            resident((D_MODEL, D_MODEL)),
            pl.BlockSpec((tm, D_MODEL), lambda i: (i, 0)),
        ],
        out_specs=pl.BlockSpec((tm, D_MODEL), lambda i: (i, 0)),
        out_shape=jax.ShapeDtypeStruct((m, D_MODEL), F32),
        compiler_params=pltpu.CompilerParams(
            dimension_semantics=("parallel",), vmem_limit_bytes=VMEM_LIMIT),
        name="merge_out",
    )(ya, yb, yc, proj2, proj2, proj2, wa, wb, wc, wo, x2d)


def _layer(x, mem, norm_g, w_in, a_q_g, a_k_g, a_rel_bias, w_up_a,
           b_mu_rkv, b_mu_w, b_mu_a, b_w0, b_w2, b_a0, b_a2, b_k_k, b_k_a, b_r_k,
           b_ln_w, b_ln_b, w_up_b, mem_norm_g, w_mem_kv, c_q_g, c_k_g, w_up_c, w_o):
    bsz, seq, d = x.shape
    t = bsz * seq
    x2d = x.reshape(t, d)
    w_main, w_lora = _prep_w_in(w_in, 128)
    wa, wb, wc, wo, wm = _cast_weights([w_up_a, w_up_b, w_up_c, w_o, w_mem_kv], 8)

    proj, lora = _in_proj(x2d, norm_g, w_main, w_lora, 1024, 1024)
    mkv = _norm_matmul(mem.reshape(bsz * N_MEM, d), mem_norm_g, wm, 1024, 1024, "mem_kv")

    proj3 = proj.reshape(bsz, seq, MAIN_COLS)
    ya = _band_attn(proj3, _band_bias_table(a_rel_bias), a_q_g, a_k_g)
    yc = _mem_attn(proj3, mkv.reshape(bsz, N_MEM, 2 * C_WIDTH), c_q_g, c_k_g, 512)
    yb = _rwkv(proj3, lora.reshape(bsz, seq, 2 * LORA), b_mu_rkv,
               jnp.concatenate([b_mu_w, b_mu_a]), b_w0, b_a0,
               jnp.concatenate([b_w2, b_a2], axis=0), b_k_k, b_k_a, b_r_k.reshape(-1), b_ln_w, b_ln_b, 8 * CHUNK)

    out = _merge_out(ya.reshape(t, A_WIDTH), yb.reshape(t, B_WIDTH), yc.reshape(t, C_WIDTH), proj,
                     wa, wb, wc, wo, x2d, 256)
    return out.reshape(bsz, seq, d)


def kernel(x, mem, norm_g, w_in, a_q_g, a_k_g, a_rel_bias, w_up_a, b_mu_rkv, b_mu_w, b_mu_a, b_w0, b_w2, b_a0, b_a2, b_k_k, b_k_a, b_r_k, b_ln_w, b_ln_b, w_up_b, mem_norm_g, w_mem_kv, c_q_g, c_k_g, w_up_c, w_o):
    for l in range(norm_g.shape[0]):
        x = _layer(x, mem, norm_g[l], w_in[l], a_q_g[l], a_k_g[l], a_rel_bias[l], w_up_a[l],
                   b_mu_rkv[l], b_mu_w[l], b_mu_a[l], b_w0[l], b_w2[l], b_a0[l], b_a2[l],
                   b_k_k[l], b_k_a[l], b_r_k[l], b_ln_w[l], b_ln_b[l], w_up_b[l],
                   mem_norm_g[l], w_mem_kv[l], c_q_g[l], c_k_g[l], w_up_c[l], w_o[l])
    return x
```

```python
import numpy as np
import jax
import jax.numpy as jnp
from jax import lax
from jax.experimental import pallas as pl
from jax.experimental.pallas import tpu as pltpu

D_MODEL = 2048
CHUNK = 64
N_MEM = 256
NORM_EPS = 1e-6
NEG_INF = -1e30

A_HEADS = 8
A_HEAD_DIM = 128
A_WIDTH = A_HEADS * A_HEAD_DIM
A_LEFT_CHUNKS = 8
REL_CLIP = 128

B_HEADS = 16
B_HEAD_DIM = 64
B_WIDTH = B_HEADS * B_HEAD_DIM
LORA = 64
GN_EPS = 64e-5

C_HEADS = 4
C_HEAD_DIM = 256
C_WIDTH = C_HEADS * C_HEAD_DIM

LANES = 128
VMEM_LIMIT = 56 * 1024 * 1024

A_BAND = (A_LEFT_CHUNKS + 1) * CHUNK
K_WINDOW = -(-A_BAND // LANES) * LANES
K_SLACK = K_WINDOW - A_BAND
K_PAD = K_WINDOW - CHUNK
ATTN_CHUNKS_IN_FLIGHT = 16
NORM_SUB_ROWS = 256

COL_AQ, COL_AK, COL_AV, COL_AZ = 0, 1, 2, 3
COL_BR, COL_BK, COL_BV, COL_BZ = 4, 5, 6, 7
COL_CQ, COL_CZ = 8, 9
COL_GA, COL_GB, COL_GC = 10, 12, 14
MAIN_COLS = 16 * 1024
LORA_LO = 4 * A_WIDTH + 4 * B_WIDTH
LORA_HI = LORA_LO + 2 * LORA

BF16 = jnp.bfloat16
F32 = jnp.float32


def _bdot(a, b):
    return jnp.dot(a.astype(BF16), b.astype(BF16), preferred_element_type=F32)


def _bdot_nt(a, b):
    return lax.dot_general(a.astype(BF16), b.astype(BF16), (((1,), (1,)), ((), ())),
                           preferred_element_type=F32)


def _bdot_tn(a, b):
    return lax.dot_general(a.astype(BF16), b.astype(BF16), (((0,), (0,)), ((), ())),
                           preferred_element_type=F32)


def _split2(x):
    hi = x.astype(BF16)
    lo = (x - hi.astype(F32)).astype(BF16)
    return hi, lo


def _split3(x):
    hi = x.astype(BF16)
    r1 = x - hi.astype(F32)
    mid = r1.astype(BF16)
    lo = (r1 - mid.astype(F32)).astype(BF16)
    return hi, mid, lo


def _silu(x):
    return x * jax.nn.sigmoid(x)


def _prep_w_in_kernel(w_ref, main_ref, lora_ref):
    main_ref[:, :LORA_LO] = w_ref[:, :LORA_LO].astype(BF16)
    main_ref[:, LORA_LO:] = w_ref[:, LORA_HI:].astype(BF16)
    lora_ref[...] = w_ref[:, LORA_LO:LORA_HI].astype(BF16)


def _prep_w_in(w_in, rows):
    k, n = w_in.shape
    return pl.pallas_call(
        _prep_w_in_kernel,
        grid=(k // rows,),
        in_specs=[pl.BlockSpec((rows, n), lambda i: (i, 0))],
        out_specs=[
            pl.BlockSpec((rows, MAIN_COLS), lambda i: (i, 0)),
            pl.BlockSpec((rows, 2 * LORA), lambda i: (i, 0)),
        ],
        out_shape=[
            jax.ShapeDtypeStruct((k, MAIN_COLS), BF16),
            jax.ShapeDtypeStruct((k, 2 * LORA), BF16),
        ],
        compiler_params=pltpu.CompilerParams(
            dimension_semantics=("parallel",), vmem_limit_bytes=VMEM_LIMIT),
        name="prep_w_in",
    )(w_in)


def _cast_weights_kernel(*refs):
    n = len(refs) // 2
    for src, dst in zip(refs[:n], refs[n:]):
        dst[...] = src[...].astype(BF16)


def _cast_weights(ws, n_steps):
    specs = [pl.BlockSpec((w.shape[0] // n_steps, w.shape[1]), lambda i: (i, 0)) for w in ws]
    return pl.pallas_call(
        _cast_weights_kernel,
        grid=(n_steps,),
        in_specs=specs,
        out_specs=specs,
        out_shape=[jax.ShapeDtypeStruct(w.shape, BF16) for w in ws],
        compiler_params=pltpu.CompilerParams(
            dimension_semantics=("parallel",), vmem_limit_bytes=VMEM_LIMIT),
        name="cast_weights",
    )(*ws)


def _norm_matmul_kernel(x_ref, g_ref, w_ref, o_ref, h_ref):
    @pl.when(pl.program_id(1) == 0)
    def _():
        x = x_ref[...]
        ms = jnp.mean(x * x, axis=-1, keepdims=True)
        h_ref[...] = (x * lax.rsqrt(ms + NORM_EPS) * g_ref[...]).astype(BF16)

    o_ref[...] = jnp.dot(h_ref[...], w_ref[...], preferred_element_type=F32).astype(o_ref.dtype)


def _norm_matmul(x2d, g, w_bf16, tm, tn, name):
    m, k = x2d.shape
    n = w_bf16.shape[1]
    return pl.pallas_call(
        _norm_matmul_kernel,
        grid=(m // tm, n // tn),
        in_specs=[
            pl.BlockSpec((tm, k), lambda i, j: (i, 0)),
            pl.BlockSpec((1, k), lambda i, j: (0, 0)),
            pl.BlockSpec((k, tn), lambda i, j: (0, j)),
        ],
        out_specs=pl.BlockSpec((tm, tn), lambda i, j: (i, j)),
        out_shape=jax.ShapeDtypeStruct((m, n), F32),
        scratch_shapes=[pltpu.VMEM((tm, k), BF16)],
        compiler_params=pltpu.CompilerParams(
            dimension_semantics=("parallel", "arbitrary"), vmem_limit_bytes=VMEM_LIMIT),
        name=name,
    )(x2d, g.reshape(1, k), w_bf16)


def _in_proj_kernel(x_ref, g_ref, w_ref, wla_ref, o_ref, la_ref, h_ref):
    j = pl.program_id(1)

    @pl.when(j == 0)
    def _():
        for r0 in range(0, x_ref.shape[0], NORM_SUB_ROWS):
            rows = slice(r0, r0 + NORM_SUB_ROWS)
            x = x_ref[rows, :]
            ms = jnp.mean(x * x, axis=-1, keepdims=True)
            h = (x * lax.rsqrt(ms + NORM_EPS) * g_ref[...]).astype(BF16)
            h_ref[rows, :] = h
            la_ref[rows, :] = jnp.dot(h, wla_ref[...], preferred_element_type=F32)
            o_ref[rows, :] = jnp.dot(h, w_ref[...], preferred_element_type=F32)

    @pl.when(j != 0)
    def _():
        o_ref[...] = jnp.dot(h_ref[...], w_ref[...], preferred_element_type=F32)


def _in_proj(x2d, g, w_main, w_lora, tm, tn):
    m, k = x2d.shape
    n = w_main.shape[1]
    return pl.pallas_call(
        _in_proj_kernel,
        grid=(m // tm, n // tn),
        in_specs=[
            pl.BlockSpec((tm, k), lambda i, j: (i, 0)),
            pl.BlockSpec((1, k), lambda i, j: (0, 0)),
            pl.BlockSpec((k, tn), lambda i, j: (0, j)),
            pl.BlockSpec((k, 2 * LORA), lambda i, j: (0, 0)),
        ],
        out_specs=[
            pl.BlockSpec((tm, tn), lambda i, j: (i, j)),
            pl.BlockSpec((tm, 2 * LORA), lambda i, j: (i, 0)),
        ],
        out_shape=[
            jax.ShapeDtypeStruct((m, n), F32),
            jax.ShapeDtypeStruct((m, 2 * LORA), F32),
        ],
        scratch_shapes=[pltpu.VMEM((tm, k), BF16)],
        compiler_params=pltpu.CompilerParams(
            dimension_semantics=("parallel", "arbitrary"), vmem_limit_bytes=VMEM_LIMIT),
        name="in_proj",
    )(x2d, g.reshape(1, k), w_main, w_lora)


def _band_attn_kernel(q_ref, k_ref, v_ref, z_ref, bias_ref, gq_ref, gk_ref, o_ref,
                      qn_ref, kp_ref, vp_ref):
    seq = q_ref.shape[0]
    scale = A_HEAD_DIM ** -0.5
    q = q_ref[...]
    qn_ref[...] = (q * lax.rsqrt(jnp.mean(q * q, axis=-1, keepdims=True) + NORM_EPS)).astype(BF16)
    k = k_ref[...]
    kn = k * lax.rsqrt(jnp.mean(k * k, axis=-1, keepdims=True) + NORM_EPS) * (gk_ref[...] * gq_ref[...] * scale)
    kp_ref[0:K_PAD, :] = jnp.zeros((K_PAD, A_HEAD_DIM), BF16)
    vp_ref[0:K_PAD, :] = jnp.zeros((K_PAD, A_HEAD_DIM), BF16)
    kp_ref[K_PAD:, :] = kn.astype(BF16)
    vp_ref[K_PAD:, :] = v_ref[...].astype(BF16)
    def scores(c):
        q0 = c * CHUNK
        first = K_PAD - q0
        skip = max(first, 0) // LANES * LANES
        s = _bdot_nt(qn_ref[q0:q0 + CHUNK, :], kp_ref[q0 + skip:q0 + K_WINDOW, :]) + bias_ref[:, skip:]
        if first > skip:
            col = lax.broadcasted_iota(jnp.int32, s.shape, 1)
            s = jnp.where(col >= first - skip, s, NEG_INF)
        return s, skip

    for c0 in range(0, seq // CHUNK, ATTN_CHUNKS_IN_FLIGHT):
        group = range(c0, c0 + ATTN_CHUNKS_IN_FLIGHT)
        sc = [scores(c) for c in group]
        mx = [jnp.max(s, axis=-1, keepdims=True) for s, _ in sc]
        ps = [jnp.exp(s - m) for (s, _), m in zip(sc, mx)]
        ls = [jnp.sum(p, axis=-1, keepdims=True) for p in ps]
        outs = [_bdot(p, vp_ref[c * CHUNK + skip:c * CHUNK + K_WINDOW, :]) / l
                for c, p, l, (_, skip) in zip(group, ps, ls, sc)]
        for c, o in zip(group, outs):
            q0 = c * CHUNK
            o_ref[q0:q0 + CHUNK, :] = (o * _silu(z_ref[q0:q0 + CHUNK, :])).astype(o_ref.dtype)


def _band_bias_table(rel_bias):
    period = 1024
    q = np.arange(period)
    m_minus_r = np.where(q < K_WINDOW, q, q - period)
    dist = K_PAD - m_minus_r
    f = rel_bias[:, np.clip(dist, -REL_CLIP, REL_CLIP) + REL_CLIP].astype(F32)
    seq = jnp.tile(f, (1, CHUNK))[:, :CHUNK * (period - 1)]
    toeplitz = seq.reshape(-1, CHUNK, period - 1)[:, :, :K_WINDOW]
    in_band = np.arange(K_WINDOW)[None, None, :] >= K_SLACK
    return jnp.where(jnp.asarray(in_band), toeplitz, NEG_INF)


def _band_attn(proj3, bias_tab, gq, gk):
    bsz, seq, _ = proj3.shape
    hb = A_WIDTH // A_HEAD_DIM

    def col(group):
        return pl.BlockSpec((None, seq, A_HEAD_DIM), lambda b, h: (b, 0, group * hb + h))

    return pl.pallas_call(
        _band_attn_kernel,
        grid=(bsz, A_HEADS),
        in_specs=[
            col(COL_AQ), col(COL_AK), col(COL_AV), col(COL_AZ),
            pl.BlockSpec((None, CHUNK, K_WINDOW), lambda b, h: (h, 0, 0)),
            pl.BlockSpec((1, A_HEAD_DIM), lambda b, h: (0, 0)),
            pl.BlockSpec((1, A_HEAD_DIM), lambda b, h: (0, 0)),
        ],
        out_specs=pl.BlockSpec((None, seq, A_HEAD_DIM), lambda b, h: (b, 0, h)),
        out_shape=jax.ShapeDtypeStruct((bsz, seq, A_WIDTH), BF16),
        scratch_shapes=[
            pltpu.VMEM((seq, A_HEAD_DIM), BF16),
            pltpu.VMEM((seq + K_PAD, A_HEAD_DIM), BF16),
            pltpu.VMEM((seq + K_PAD, A_HEAD_DIM), BF16),
        ],
        compiler_params=pltpu.CompilerParams(
            dimension_semantics=("parallel", "parallel"), vmem_limit_bytes=VMEM_LIMIT),
        name="band_attn",
    )(proj3, proj3, proj3, proj3, bias_tab, gq.reshape(1, -1), gk.reshape(1, -1))


def _mem_attn_kernel(q_ref, z_ref, mk_ref, mv_ref, gq_ref, gk_ref, o_ref, kn_ref):
    scale = C_HEAD_DIM ** -0.5
    heads = [slice(h * C_HEAD_DIM, (h + 1) * C_HEAD_DIM) for h in range(C_HEADS)]

    def unit_rms(x):
        return x * lax.rsqrt(jnp.mean(x * x, axis=-1, keepdims=True) + NORM_EPS)

    @pl.when(pl.program_id(1) == 0)
    def _():
        gain = gk_ref[...] * gq_ref[...] * scale
        for sl in heads:
            kn_ref[:, sl] = (unit_rms(mk_ref[:, sl]) * gain).astype(BF16)

    qn = [unit_rms(q_ref[:, sl]) for sl in heads]
    kn = [kn_ref[:, sl] for sl in heads]
    sc = [_bdot_nt(q, k) for q, k in zip(qn, kn)]
    mx = [jnp.max(s, axis=-1, keepdims=True) for s in sc]
    ps = [jnp.exp(s - m) for s, m in zip(sc, mx)]
    ls = [jnp.sum(p, axis=-1, keepdims=True) for p in ps]
    outs = [_bdot(p, mv_ref[:, sl]) / l for p, l, sl in zip(ps, ls, heads)]
    for o, sl in zip(outs, heads):
        o_ref[:, sl] = (o * _silu(z_ref[:, sl])).astype(o_ref.dtype)


def _mem_attn(proj3, mkv3, gq, gk, ts):
    bsz, seq, _ = proj3.shape
    return pl.pallas_call(
        _mem_attn_kernel,
        grid=(bsz, seq // ts),
        in_specs=[
            pl.BlockSpec((None, ts, C_WIDTH), lambda b, s: (b, s, COL_CQ)),
            pl.BlockSpec((None, ts, C_WIDTH), lambda b, s: (b, s, COL_CZ)),
            pl.BlockSpec((None, N_MEM, C_WIDTH), lambda b, s: (b, 0, 0)),
            pl.BlockSpec((None, N_MEM, C_WIDTH), lambda b, s: (b, 0, 1)),
            pl.BlockSpec((1, C_HEAD_DIM), lambda b, s: (0, 0)),
            pl.BlockSpec((1, C_HEAD_DIM), lambda b, s: (0, 0)),
        ],
        out_specs=pl.BlockSpec((None, ts, C_WIDTH), lambda b, s: (b, s, 0)),
        out_shape=jax.ShapeDtypeStruct((bsz, seq, C_WIDTH), BF16),
        scratch_shapes=[pltpu.VMEM((N_MEM, C_WIDTH), BF16)],
        compiler_params=pltpu.CompilerParams(
            dimension_semantics=("parallel", "arbitrary"), vmem_limit_bytes=VMEM_LIMIT),
        name="mem_attn",
    )(proj3, proj3, mkv3, mkv3, gq.reshape(1, -1), gk.reshape(1, -1))


def _shift_rows(x, carry_row):
    rolled = pltpu.roll(x, 1, 0)
    row = lax.broadcasted_iota(jnp.int32, x.shape, 0)
    return jnp.where(row == 0, carry_row, rolled)


def _seg_sum(x, seg_ones2):
    hi, lo = _split2(x)
    return jnp.dot(jnp.concatenate([hi, lo], axis=1), seg_ones2, preferred_element_type=F32)


def _drain(gen):
    for _ in gen:
        pass


def _rwkv_kernel(pr_ref, pk_ref, pv_ref, pz_ref, la_ref,
                 mur_ref, muk_ref, muv_ref, mula_ref, w0_ref, a0_ref, w2a2_ref,
                 kk_ref, ka_ref, rk_ref, lnw_ref, lnb_ref,
                 o_ref,
                 st_ref, cr_ref, ck_ref, cv_ref, cla_ref):
    L = CHUNK
    n_chunks = pr_ref.shape[0] // L
    n_pairs = B_HEADS // 2
    pair_lanes = [slice(p * LANES, (p + 1) * LANES) for p in range(n_pairs)]

    @pl.when(pl.program_id(1) == 0)
    def _():
        st_ref[...] = jnp.zeros_like(st_ref)
        cr_ref[...] = jnp.zeros_like(cr_ref)
        ck_ref[...] = jnp.zeros_like(ck_ref)
        cv_ref[...] = jnp.zeros_like(cv_ref)
        cla_ref[...] = jnp.zeros_like(cla_ref)

    lane = lax.broadcasted_iota(jnp.int32, (L, LANES), 1)
    lo_half = lane < B_HEAD_DIM
    eye_packed = ((lane & (B_HEAD_DIM - 1)) == lax.broadcasted_iota(jnp.int32, (L, LANES), 0)).astype(F32)
    lo_half2 = (lax.broadcasted_iota(jnp.int32, (L, 2 * LANES), 1) & B_HEAD_DIM) == 0
    ti = lax.broadcasted_iota(jnp.int32, (L, L), 0)
    si = lax.broadcasted_iota(jnp.int32, (L, L), 1)
    tri = (si <= ti).astype(BF16)
    tri3 = jnp.concatenate([tri, tri, tri], axis=1)
    ri = lax.broadcasted_iota(jnp.int32, (LANES, LANES), 0)
    ci = lax.broadcasted_iota(jnp.int32, (LANES, LANES), 1)
    same_head = (ri < B_HEAD_DIM) == (ci < B_HEAD_DIM)
    seg_ones = jnp.concatenate([same_head.astype(BF16)] * 2, axis=0)
    inv_n = 1.0 / B_HEAD_DIM
    mi = lax.broadcasted_iota(jnp.int32, (2 * L, 4 * L), 0)
    mj = lax.broadcasted_iota(jnp.int32, (2 * L, 4 * L), 1)
    tt, ss = mi & (L - 1), mj & (L - 1)
    tri_mask = (ss < tt) | ((mi >= L) & (ss == tt))
    w_hi, w_lo = _split2(w2a2_ref[...])
    w_hhl = jnp.concatenate([w_hi, w_hi, w_lo], axis=0)

    def stack_heads(x2):
        lo = lo_half if x2.shape[1] == LANES else lo_half2
        xb = x2.astype(BF16)
        return jnp.concatenate([jnp.where(lo, xb, 0.0), jnp.where(lo, 0.0, xb)], axis=0).astype(BF16)

    tb = la_ref.shape[0]
    la_all = la_ref[...]
    la_all = la_all + mula_ref[...] * (_shift_rows(la_all, cla_ref[0:1, :]) - la_all)
    is_wd = lax.broadcasted_iota(jnp.int32, (tb, LANES), 1) < LORA
    x_hi, x_lo = _split2(jnp.concatenate([jnp.where(is_wd, jnp.tanh(la_all), 0.0),
                                          jnp.where(is_wd, 0.0, la_all)], axis=0))
    lora_up = jnp.dot(jnp.concatenate([x_hi, x_lo, x_hi], axis=1), w_hhl, preferred_element_type=F32)

    prev_row = [cr_ref[0:1, :], ck_ref[0:1, :], cv_ref[0:1, :]]

    def prep(c, out):
        rows = slice(c * L, (c + 1) * L)

        def lerp(x_ref, idx, mu_ref):
            x = x_ref[rows, :]
            prev = _shift_rows(x, prev_row[idx])
            prev_row[idx] = x[L - 1:L, :]
            return x + mu_ref[...] * (prev - x)

        r = lerp(pr_ref, 0, mur_ref)
        k = lerp(pk_ref, 1, muk_ref)
        v = lerp(pv_ref, 2, muv_ref)
        zw = -(w0_ref[...] + lora_up[rows, :])
        softplus = jnp.maximum(zw, 0.0) + jnp.log(1.0 + jnp.exp(-jnp.abs(zw)))
        lw = -jnp.exp(-softplus - 0.5)
        a_sig = jax.nn.sigmoid(a0_ref[...] + lora_up[tb + c * L:tb + (c + 1) * L, :])
        yield
        cum = jnp.dot(tri3, jnp.concatenate(_split3(lw), axis=0), preferred_element_type=F32)
        e_pos = jnp.exp(cum)
        e_neg = jnp.exp(-cum)
        e_prev = jnp.exp(cum - lw)
        yield
        for p, sl in enumerate(pair_lanes):
            kk_raw = k[:, sl] * kk_ref[:, sl]
            kmod = k[:, sl] * (1.0 + (a_sig[:, sl] - 1.0) * ka_ref[:, sl])
            sums = _seg_sum(jnp.concatenate([kk_raw * kk_raw, r[:, sl] * kmod * rk_ref[:, sl]], axis=0), seg_ones)
            kk = kk_raw * lax.rsqrt(jnp.maximum(sums[:L], 1e-24))
            bt = (kk * a_sig[:, sl]) * e_neg[:, sl]
            kt = kmod * e_neg[:, sl]
            out.append(dict(
                at=(-kk) * e_prev[:, sl], rt=r[:, sl] * e_pos[:, sl], bt=bt, kt=kt,
                bks=jnp.concatenate([stack_heads(bt), stack_heads(kt)], axis=0),
                vb=v[:, sl].astype(BF16), vs=stack_heads(v[:, sl]),
                bonus=sums[L:] * v[:, sl], decay=e_pos[L - 1:L, sl]))
            if p % 2 == 1:
                yield

    def scores(ops):
        for d in ops:
            d["sm"] = jnp.where(tri_mask, _bdot_nt(jnp.concatenate([d["at"], d["rt"]], axis=0), d["bks"]),
                                0.0).astype(BF16)
        for d in ops:
            d["av"] = jnp.dot(d["sm"][:, 2 * L:], d["vs"], preferred_element_type=F32)
            d["x"] = jnp.concatenate([d["at"], d["av"][:L]], axis=1)
            d["pw"] = d["sm"][:L, :2 * L]

    def post(c, ops, o2):
        rows = slice(c * L, (c + 1) * L)
        for p, sl in enumerate(pair_lanes):
            mean = _seg_sum(o2[p], seg_ones) * inv_n
            dev = o2[p] - mean
            var = _seg_sum(dev * dev, seg_ones) * inv_n
            on = dev * lax.rsqrt(var + GN_EPS) * lnw_ref[:, sl] + lnb_ref[:, sl]
            o_ref[rows, sl] = ((on + ops[p]["bonus"]) * _silu(pz_ref[rows, sl])).astype(o_ref.dtype)
            if p % 2 == 1:
                yield

    st = [st_ref[p] for p in range(n_pairs)]
    ops = [[] for _ in range(n_chunks)]
    _drain(prep(0, ops[0]))
    scores(ops[0])
    side = []
    for c in range(n_chunks):
        cur = ops[c]
        if c + 1 < n_chunks:
            side.append(prep(c + 1, ops[c + 1]))
        def spread():
            for gen in side:
                next(gen, None)

        for d in cur:
            d["t"] = eye_packed + d["pw"].astype(F32)
            d["pw"] = jnp.dot(d["pw"], stack_heads(d["pw"]), preferred_element_type=F32).astype(BF16)
        spread()
        n = 2
        while 2 * n < L:
            for d in cur:
                both = jnp.dot(jnp.concatenate([d["pw"], d["t"].astype(BF16)], axis=0), stack_heads(d["pw"]),
                               preferred_element_type=F32)
                d["t"] = d["t"] + both[L:]
                d["pw"] = both[:L].astype(BF16)
            n *= 2
            spread()
        for d in cur:
            d["t"] = d["t"] + jnp.dot(d["t"].astype(BF16), stack_heads(d["pw"]), preferred_element_type=F32)
        spread()
        for d in cur:
            d["x"] = jnp.dot(d["t"].astype(BF16), stack_heads(d["x"]), preferred_element_type=F32)
        for gen in side:
            _drain(gen)
        side = []
        if c + 1 < n_chunks:
            scores(ops[c + 1])
        hs = [_bdot_nt(jnp.concatenate([d["x"][:, :LANES], d["rt"]], axis=0), st[p]) for p, d in enumerate(cur)]
        us = [hs[p][:L] + d["x"][:, LANES:] for p, d in enumerate(cur)]
        o2 = [hs[p][L:] + jnp.dot(d["sm"][L:, :2 * L], stack_heads(us[p]), preferred_element_type=F32)
              + d["av"][L:] for p, d in enumerate(cur)]
        for p, d in enumerate(cur):
            upd = _bdot_tn(jnp.concatenate([us[p].astype(BF16), d["vb"]], axis=0),
                           jnp.concatenate([d["bt"], d["kt"]], axis=0))
            st[p] = (st[p] + jnp.where(same_head, upd, 0.0)) * d["decay"]
        side.append(post(c, cur, o2))
    for gen in side:
        _drain(gen)

    for p in range(n_pairs):
        st_ref[p] = st[p]
    cr_ref[0:1, :], ck_ref[0:1, :], cv_ref[0:1, :] = prev_row
    cla_ref[0:1, :] = la_ref[tb - 1:tb, :]


def _rwkv(proj3, lora3, mu_rkv, mu_la, w0, a0, w2a2, k_k, k_a, r_k, ln_w, ln_b, tokens_per_step):
    bsz, seq, _ = proj3.shape
    tb = tokens_per_step

    def col(group):
        return pl.BlockSpec((None, tb, B_WIDTH), lambda b, c: (b, c, group))

    def vec(n):
        return pl.BlockSpec((1, n), lambda b, c: (0, 0))

    row = lambda t: t.reshape(1, -1)
    return pl.pallas_call(
        _rwkv_kernel,
        grid=(bsz, seq // tb),
        in_specs=[
            col(COL_BR), col(COL_BK), col(COL_BV), col(COL_BZ),
            pl.BlockSpec((None, tb, 2 * LORA), lambda b, c: (b, c, 0)),
            vec(B_WIDTH), vec(B_WIDTH), vec(B_WIDTH), vec(2 * LORA), vec(B_WIDTH), vec(B_WIDTH),
            pl.BlockSpec((2 * LORA, B_WIDTH), lambda b, c: (0, 0)),
            vec(B_WIDTH), vec(B_WIDTH), vec(B_WIDTH), vec(B_WIDTH), vec(B_WIDTH),
        ],
        out_specs=pl.BlockSpec((None, tb, B_WIDTH), lambda b, c: (b, c, 0)),
        out_shape=jax.ShapeDtypeStruct((bsz, seq, B_WIDTH), BF16),
        scratch_shapes=[
            pltpu.VMEM((B_HEADS // 2, LANES, LANES), F32),
            pltpu.VMEM((8, B_WIDTH), F32), pltpu.VMEM((8, B_WIDTH), F32), pltpu.VMEM((8, B_WIDTH), F32),
            pltpu.VMEM((8, 2 * LORA), F32),
        ],
        compiler_params=pltpu.CompilerParams(
            dimension_semantics=("parallel", "arbitrary"), vmem_limit_bytes=VMEM_LIMIT),
        name="rwkv",
    )(proj3, proj3, proj3, proj3, lora3,
      row(mu_rkv[0]), row(mu_rkv[1]), row(mu_rkv[2]), row(mu_la), row(w0), row(a0), w2a2,
      row(k_k), row(k_a), row(r_k), row(ln_w), row(ln_b))


def _merge_out_kernel(ya_ref, yb_ref, yc_ref, ga_ref, gb_ref, gc_ref, wa_ref, wb_ref, wc_ref, wo_ref, x_ref,
                      o_ref):
    merged = (jax.nn.sigmoid(ga_ref[...]) * jnp.dot(ya_ref[...], wa_ref[...], preferred_element_type=F32)
              + jax.nn.sigmoid(gb_ref[...]) * jnp.dot(yb_ref[...], wb_ref[...], preferred_element_type=F32)
              + jax.nn.sigmoid(gc_ref[...]) * jnp.dot(yc_ref[...], wc_ref[...], preferred_element_type=F32))
    o_ref[...] = x_ref[...] + jnp.dot(merged.astype(BF16), wo_ref[...], preferred_element_type=F32)


def _merge_out(ya, yb, yc, proj2, wa, wb, wc, wo, x2d, tm):
    m = x2d.shape[0]

    def ybranch(width):
        return pl.BlockSpec((tm, width), lambda i: (i, 0))

    def gate(group):
        return pl.BlockSpec((tm, D_MODEL), lambda i: (i, group * 1024 // D_MODEL))

    def resident(shape):
        return pl.BlockSpec(shape, lambda i: (0, 0), pipeline_mode=pl.Buffered(1))

    return pl.pallas_call(
        _merge_out_kernel,
        grid=(m // tm,),
        in_specs=[
            ybranch(A_WIDTH), ybranch(B_WIDTH), ybranch(C_WIDTH),
            gate(COL_GA), gate(COL_GB), gate(COL_GC),
            resident((A_WIDTH, D_MODEL)), resident((B_WIDTH, D_MODEL)), resident((C_WIDTH, D_MODEL)),
            resident((D_MODEL, D_MODEL)),
            pl.BlockSpec((tm, D_MODEL), lambda i: (i, 0)),
        ],
        out_specs=pl.BlockSpec((tm, D_MODEL), lambda i: (i, 0)),
        out_shape=jax.ShapeDtypeStruct((m, D_MODEL), F32),
        compiler_params=pltpu.CompilerParams(
            dimension_semantics=("parallel",), vmem_limit_bytes=VMEM_LIMIT),
        name="merge_out",
    )(ya, yb, yc, proj2, proj2, proj2, wa, wb, wc, wo, x2d)


def _layer(x, mem, norm_g, w_in, a_q_g, a_k_g, a_rel_bias, w_up_a,
           b_mu_rkv, b_mu_w, b_mu_a, b_w0, b_w2, b_a0, b_a2, b_k_k, b_k_a, b_r_k,
           b_ln_w, b_ln_b, w_up_b, mem_norm_g, w_mem_kv, c_q_g, c_k_g, w_up_c, w_o):
    bsz, seq, d = x.shape
    t = bsz * seq
    x2d = x.reshape(t, d)
    w_main, w_lora = _prep_w_in(w_in, 128)
    wa, wb, wc, wo, wm = _cast_weights([w_up_a, w_up_b, w_up_c, w_o, w_mem_kv], 8)

    proj, lora = _in_proj(x2d, norm_g, w_main, w_lora, 1024, 1024)
    mkv = _norm_matmul(mem.reshape(bsz * N_MEM, d), mem_norm_g, wm, 1024, 1024, "mem_kv")

    proj3 = proj.reshape(bsz, seq, MAIN_COLS)
    ya = _band_attn(proj3, _band_bias_table(a_rel_bias), a_q_g, a_k_g)
    yc = _mem_attn(proj3, mkv.reshape(bsz, N_MEM, 2 * C_WIDTH), c_q_g, c_k_g, 512)
    yb = _rwkv(proj3, lora.reshape(bsz, seq, 2 * LORA), b_mu_rkv,
               jnp.concatenate([b_mu_w, b_mu_a]), b_w0, b_a0,
               jnp.concatenate([b_w2, b_a2], axis=0), b_k_k, b_k_a, b_r_k.reshape(-1), b_ln_w, b_ln_b, 8 * CHUNK)

    out = _merge_out(ya.reshape(t, A_WIDTH), yb.reshape(t, B_WIDTH), yc.reshape(t, C_WIDTH), proj,
                     wa, wb, wc, wo, x2d, 256)
    return out.reshape(bsz, seq, d)


def kernel(x, mem, norm_g, w_in, a_q_g, a_k_g, a_rel_bias, w_up_a, b_mu_rkv, b_mu_w, b_mu_a, b_w0, b_w2, b_a0, b_a2, b_k_k, b_k_a, b_r_k, b_ln_w, b_ln_b, w_up_b, mem_norm_g, w_mem_kv, c_q_g, c_k_g, w_up_c, w_o):
    for l in range(norm_g.shape[0]):
        x = _layer(x, mem, norm_g[l], w_in[l], a_q_g[l], a_k_g[l], a_rel_bias[l], w_up_a[l],
                   b_mu_rkv[l], b_mu_w[l], b_mu_a[l], b_w0[l], b_w2[l], b_a0[l], b_a2[l],
                   b_k_k[l], b_k_a[l], b_r_k[l], b_ln_w[l], b_ln_b[l], w_up_b[l],
                   mem_norm_g[l], w_mem_kv[l], c_q_g[l], c_k_g[l], w_up_c[l], w_o[l])
    return x
```

```python
import numpy as np
import jax
import jax.numpy as jnp
from jax import lax
from jax.experimental import pallas as pl
from jax.experimental.pallas import tpu as pltpu

D_MODEL = 2048
CHUNK = 64
N_MEM = 256
NORM_EPS = 1e-6
NEG_INF = -1e30

A_HEADS = 8
A_HEAD_DIM = 128
A_WIDTH = A_HEADS * A_HEAD_DIM
A_LEFT_CHUNKS = 8
REL_CLIP = 128

B_HEADS = 16
B_HEAD_DIM = 64
B_WIDTH = B_HEADS * B_HEAD_DIM
LORA = 64
GN_EPS = 64e-5
DECAY_SCALE = float(np.exp(-0.5))

C_HEADS = 4
C_HEAD_DIM = 256
C_WIDTH = C_HEADS * C_HEAD_DIM

LANES = 128
VMEM_LIMIT = 56 * 1024 * 1024

A_BAND = (A_LEFT_CHUNKS + 1) * CHUNK
K_WINDOW = -(-A_BAND // LANES) * LANES
K_SLACK = K_WINDOW - A_BAND
K_PAD = K_WINDOW - CHUNK
ATTN_CHUNKS_IN_FLIGHT = 16
NORM_SUB_ROWS = 256

COL_AQ, COL_AK, COL_AV, COL_AZ = 0, 1, 2, 3
COL_BR, COL_BK, COL_BV, COL_BZ = 4, 5, 6, 7
COL_CQ, COL_CZ = 8, 9
COL_GA, COL_GB, COL_GC = 10, 12, 14
MAIN_COLS = 16 * 1024
LORA_LO = 4 * A_WIDTH + 4 * B_WIDTH
LORA_HI = LORA_LO + 2 * LORA

BF16 = jnp.bfloat16
F32 = jnp.float32


def _bdot(a, b):
    return jnp.dot(a.astype(BF16), b.astype(BF16), preferred_element_type=F32)


def _bdot_nt(a, b):
    return lax.dot_general(a.astype(BF16), b.astype(BF16), (((1,), (1,)), ((), ())),
                           preferred_element_type=F32)


def _bdot_tn(a, b):
    return lax.dot_general(a.astype(BF16), b.astype(BF16), (((0,), (0,)), ((), ())),
                           preferred_element_type=F32)


def _split2(x):
    hi = x.astype(BF16)
    lo = (x - hi.astype(F32)).astype(BF16)
    return hi, lo


def _split3(x):
    hi = x.astype(BF16)
    r1 = x - hi.astype(F32)
    mid = r1.astype(BF16)
    lo = (r1 - mid.astype(F32)).astype(BF16)
    return hi, mid, lo


def _silu(x):
    return x * jax.nn.sigmoid(x)


def _prep_w_in_kernel(w_ref, main_ref, lora_ref):
    main_ref[:, :LORA_LO] = w_ref[:, :LORA_LO].astype(BF16)
    main_ref[:, LORA_LO:] = w_ref[:, LORA_HI:].astype(BF16)
    lora_ref[...] = w_ref[:, LORA_LO:LORA_HI].astype(BF16)


def _prep_w_in(w_in, rows):
    k, n = w_in.shape
    return pl.pallas_call(
        _prep_w_in_kernel,
        grid=(k // rows,),
        in_specs=[pl.BlockSpec((rows, n), lambda i: (i, 0))],
        out_specs=[
            pl.BlockSpec((rows, MAIN_COLS), lambda i: (i, 0)),
            pl.BlockSpec((rows, 2 * LORA), lambda i: (i, 0)),
        ],
        out_shape=[
            jax.ShapeDtypeStruct((k, MAIN_COLS), BF16),
            jax.ShapeDtypeStruct((k, 2 * LORA), BF16),
        ],
        compiler_params=pltpu.CompilerParams(
            dimension_semantics=("parallel",), vmem_limit_bytes=VMEM_LIMIT),
        name="prep_w_in",
    )(w_in)


def _cast_weights_kernel(*refs):
    n = len(refs) // 2
    for src, dst in zip(refs[:n], refs[n:]):
        dst[...] = src[...].astype(BF16)


def _cast_weights(ws, n_steps):
    specs = [pl.BlockSpec((w.shape[0] // n_steps, w.shape[1]), lambda i: (i, 0)) for w in ws]
    return pl.pallas_call(
        _cast_weights_kernel,
        grid=(n_steps,),
        in_specs=specs,
        out_specs=specs,
        out_shape=[jax.ShapeDtypeStruct(w.shape, BF16) for w in ws],
        compiler_params=pltpu.CompilerParams(
            dimension_semantics=("parallel",), vmem_limit_bytes=VMEM_LIMIT),
        name="cast_weights",
    )(*ws)


def _norm_matmul_kernel(x_ref, g_ref, w_ref, o_ref, h_ref):
    @pl.when(pl.program_id(1) == 0)
    def _():
        x = x_ref[...]
        ms = jnp.mean(x * x, axis=-1, keepdims=True)
        h_ref[...] = (x * lax.rsqrt(ms + NORM_EPS) * g_ref[...]).astype(BF16)

    o_ref[...] = jnp.dot(h_ref[...], w_ref[...], preferred_element_type=F32).astype(o_ref.dtype)


def _norm_matmul(x2d, g, w_bf16, tm, tn, name):
    m, k = x2d.shape
    n = w_bf16.shape[1]
    return pl.pallas_call(
        _norm_matmul_kernel,
        grid=(m // tm, n // tn),
        in_specs=[
            pl.BlockSpec((tm, k), lambda i, j: (i, 0)),
            pl.BlockSpec((1, k), lambda i, j: (0, 0)),
            pl.BlockSpec((k, tn), lambda i, j: (0, j)),
        ],
        out_specs=pl.BlockSpec((tm, tn), lambda i, j: (i, j)),
        out_shape=jax.ShapeDtypeStruct((m, n), F32),
        scratch_shapes=[pltpu.VMEM((tm, k), BF16)],
        compiler_params=pltpu.CompilerParams(
            dimension_semantics=("parallel", "arbitrary"), vmem_limit_bytes=VMEM_LIMIT),
        name=name,
    )(x2d, g.reshape(1, k), w_bf16)


def _in_proj_kernel(x_ref, g_ref, w_ref, wla_ref, o_ref, la_ref, h_ref):
    j = pl.program_id(1)

    @pl.when(j == 0)
    def _():
        for r0 in range(0, x_ref.shape[0], NORM_SUB_ROWS):
            rows = slice(r0, r0 + NORM_SUB_ROWS)
            x = x_ref[rows, :]
            ms = jnp.mean(x * x, axis=-1, keepdims=True)
            h = (x * lax.rsqrt(ms + NORM_EPS) * g_ref[...]).astype(BF16)
            h_ref[rows, :] = h
            la_ref[rows, :] = jnp.dot(h, wla_ref[...], preferred_element_type=F32)
            o_ref[rows, :] = jnp.dot(h, w_ref[...], preferred_element_type=F32)

    @pl.when(j != 0)
    def _():
        o_ref[...] = jnp.dot(h_ref[...], w_ref[...], preferred_element_type=F32)


def _in_proj(x2d, g, w_main, w_lora, tm, tn):
    m, k = x2d.shape
    n = w_main.shape[1]
    return pl.pallas_call(
        _in_proj_kernel,
        grid=(m // tm, n // tn),
        in_specs=[
            pl.BlockSpec((tm, k), lambda i, j: (i, 0)),
            pl.BlockSpec((1, k), lambda i, j: (0, 0)),
            pl.BlockSpec((k, tn), lambda i, j: (0, j)),
            pl.BlockSpec((k, 2 * LORA), lambda i, j: (0, 0)),
        ],
        out_specs=[
            pl.BlockSpec((tm, tn), lambda i, j: (i, j)),
            pl.BlockSpec((tm, 2 * LORA), lambda i, j: (i, 0)),
        ],
        out_shape=[
            jax.ShapeDtypeStruct((m, n), F32),
            jax.ShapeDtypeStruct((m, 2 * LORA), F32),
        ],
        scratch_shapes=[pltpu.VMEM((tm, k), BF16)],
        compiler_params=pltpu.CompilerParams(
            dimension_semantics=("parallel", "arbitrary"), vmem_limit_bytes=VMEM_LIMIT),
        name="in_proj",
    )(x2d, g.reshape(1, k), w_main, w_lora)


def _band_attn_kernel(q_ref, k_ref, v_ref, z_ref, bias_ref, gq_ref, gk_ref, o_ref,
                      qn_ref, kp_ref, vp_ref):
    seq = q_ref.shape[0]
    scale = A_HEAD_DIM ** -0.5
    q = q_ref[...]
    qn_ref[...] = (q * lax.rsqrt(jnp.mean(q * q, axis=-1, keepdims=True) + NORM_EPS)).astype(BF16)
    k = k_ref[...]
    kn = k * lax.rsqrt(jnp.mean(k * k, axis=-1, keepdims=True) + NORM_EPS) * (gk_ref[...] * gq_ref[...] * scale)
    kp_ref[0:K_PAD, :] = jnp.zeros((K_PAD, A_HEAD_DIM), BF16)
    vp_ref[0:K_PAD, :] = jnp.zeros((K_PAD, A_HEAD_DIM), BF16)
    kp_ref[K_PAD:, :] = kn.astype(BF16)
    vp_ref[K_PAD:, :] = v_ref[...].astype(BF16)
    def scores(c):
        q0 = c * CHUNK
        first = K_PAD - q0
        skip = max(first, 0) // LANES * LANES
        s = _bdot_nt(qn_ref[q0:q0 + CHUNK, :], kp_ref[q0 + skip:q0 + K_WINDOW, :]) + bias_ref[:, skip:]
        if first > skip:
            col = lax.broadcasted_iota(jnp.int32, s.shape, 1)
            s = jnp.where(col >= first - skip, s, NEG_INF)
        return s, skip

    for c0 in range(0, seq // CHUNK, ATTN_CHUNKS_IN_FLIGHT):
        group = range(c0, c0 + ATTN_CHUNKS_IN_FLIGHT)
        sc = [scores(c) for c in group]
        mx = [jnp.max(s, axis=-1, keepdims=True) for s, _ in sc]
        ps = [jnp.exp(s - m) for (s, _), m in zip(sc, mx)]
        ls = [jnp.sum(p, axis=-1, keepdims=True) for p in ps]
        outs = [_bdot(p, vp_ref[c * CHUNK + skip:c * CHUNK + K_WINDOW, :]) / l
                for c, p, l, (_, skip) in zip(group, ps, ls, sc)]
        for c, o in zip(group, outs):
            q0 = c * CHUNK
            o_ref[q0:q0 + CHUNK, :] = (o * _silu(z_ref[q0:q0 + CHUNK, :])).astype(o_ref.dtype)


def _band_bias_table(rel_bias):
    period = 1024
    q = np.arange(period)
    m_minus_r = np.where(q < K_WINDOW, q, q - period)
    dist = K_PAD - m_minus_r
    f = rel_bias[:, np.clip(dist, -REL_CLIP, REL_CLIP) + REL_CLIP].astype(F32)
    seq = jnp.tile(f, (1, CHUNK))[:, :CHUNK * (period - 1)]
    toeplitz = seq.reshape(-1, CHUNK, period - 1)[:, :, :K_WINDOW]
    in_band = np.arange(K_WINDOW)[None, None, :] >= K_SLACK
    return jnp.where(jnp.asarray(in_band), toeplitz, NEG_INF)


def _band_attn(proj3, bias_tab, gq, gk):
    bsz, seq, _ = proj3.shape
    hb = A_WIDTH // A_HEAD_DIM

    def col(group):
        return pl.BlockSpec((None, seq, A_HEAD_DIM), lambda b, h: (b, 0, group * hb + h))

    return pl.pallas_call(
        _band_attn_kernel,
        grid=(bsz, A_HEADS),
        in_specs=[
            col(COL_AQ), col(COL_AK), col(COL_AV), col(COL_AZ),
            pl.BlockSpec((None, CHUNK, K_WINDOW), lambda b, h: (h, 0, 0)),
            pl.BlockSpec((1, A_HEAD_DIM), lambda b, h: (0, 0)),
            pl.BlockSpec((1, A_HEAD_DIM), lambda b, h: (0, 0)),
        ],
        out_specs=pl.BlockSpec((None, seq, A_HEAD_DIM), lambda b, h: (b, 0, h)),
        out_shape=jax.ShapeDtypeStruct((bsz, seq, A_WIDTH), BF16),
        scratch_shapes=[
            pltpu.VMEM((seq, A_HEAD_DIM), BF16),
            pltpu.VMEM((seq + K_PAD, A_HEAD_DIM), BF16),
            pltpu.VMEM((seq + K_PAD, A_HEAD_DIM), BF16),
        ],
        compiler_params=pltpu.CompilerParams(
            dimension_semantics=("parallel", "parallel"), vmem_limit_bytes=VMEM_LIMIT),
        name="band_attn",
    )(proj3, proj3, proj3, proj3, bias_tab, gq.reshape(1, -1), gk.reshape(1, -1))


def _mem_attn_kernel(q_ref, z_ref, mk_ref, mv_ref, gq_ref, gk_ref, o_ref, kn_ref):
    scale = C_HEAD_DIM ** -0.5
    heads = [slice(h * C_HEAD_DIM, (h + 1) * C_HEAD_DIM) for h in range(C_HEADS)]

    def unit_rms(x):
        return x * lax.rsqrt(jnp.mean(x * x, axis=-1, keepdims=True) + NORM_EPS)

    @pl.when(pl.program_id(1) == 0)
    def _():
        gain = gk_ref[...] * gq_ref[...] * scale
        for sl in heads:
            kn_ref[:, sl] = (unit_rms(mk_ref[:, sl]) * gain).astype(BF16)

    qn = [unit_rms(q_ref[:, sl]) for sl in heads]
    kn = [kn_ref[:, sl] for sl in heads]
    sc = [_bdot_nt(q, k) for q, k in zip(qn, kn)]
    mx = [jnp.max(s, axis=-1, keepdims=True) for s in sc]
    ps = [jnp.exp(s - m) for s, m in zip(sc, mx)]
    ls = [jnp.sum(p, axis=-1, keepdims=True) for p in ps]
    outs = [_bdot(p, mv_ref[:, sl]) / l for p, l, sl in zip(ps, ls, heads)]
    for o, sl in zip(outs, heads):
        o_ref[:, sl] = (o * _silu(z_ref[:, sl])).astype(o_ref.dtype)


def _mem_attn(proj3, mkv3, gq, gk, ts):
    bsz, seq, _ = proj3.shape
    return pl.pallas_call(
        _mem_attn_kernel,
        grid=(bsz, seq // ts),
        in_specs=[
            pl.BlockSpec((None, ts, C_WIDTH), lambda b, s: (b, s, COL_CQ)),
            pl.BlockSpec((None, ts, C_WIDTH), lambda b, s: (b, s, COL_CZ)),
            pl.BlockSpec((None, N_MEM, C_WIDTH), lambda b, s: (b, 0, 0)),
            pl.BlockSpec((None, N_MEM, C_WIDTH), lambda b, s: (b, 0, 1)),
            pl.BlockSpec((1, C_HEAD_DIM), lambda b, s: (0, 0)),
            pl.BlockSpec((1, C_HEAD_DIM), lambda b, s: (0, 0)),
        ],
        out_specs=pl.BlockSpec((None, ts, C_WIDTH), lambda b, s: (b, s, 0)),
        out_shape=jax.ShapeDtypeStruct((bsz, seq, C_WIDTH), BF16),
        scratch_shapes=[pltpu.VMEM((N_MEM, C_WIDTH), BF16)],
        compiler_params=pltpu.CompilerParams(
            dimension_semantics=("parallel", "arbitrary"), vmem_limit_bytes=VMEM_LIMIT),
        name="mem_attn",
    )(proj3, proj3, mkv3, mkv3, gq.reshape(1, -1), gk.reshape(1, -1))


def _shift_rows(x, carry_row):
    rolled = pltpu.roll(x, 1, 0)
    row = lax.broadcasted_iota(jnp.int32, x.shape, 0)
    return jnp.where(row == 0, carry_row, rolled)


def _seg_sum(x, seg_ones2):
    hi, lo = _split2(x)
    return jnp.dot(jnp.concatenate([hi, lo], axis=1), seg_ones2, preferred_element_type=F32)


def _drain(gen):
    for _ in gen:
        pass


def _rwkv_kernel(pr_ref, pk_ref, pv_ref, pz_ref, la_ref,
                 mur_ref, muk_ref, muv_ref, mula_ref, w0_ref, a0_ref, w2a2_ref,
                 kk_ref, ka_ref, rk_ref, lnw_ref, lnb_ref,
                 o_ref,
                 st_ref, cr_ref, ck_ref, cv_ref, cla_ref):
    L = CHUNK
    n_chunks = pr_ref.shape[0] // L
    n_pairs = B_HEADS // 2
    pair_lanes = [slice(p * LANES, (p + 1) * LANES) for p in range(n_pairs)]

    @pl.when(pl.program_id(1) == 0)
    def _():
        st_ref[...] = jnp.zeros_like(st_ref)
        cr_ref[...] = jnp.zeros_like(cr_ref)
        ck_ref[...] = jnp.zeros_like(ck_ref)
        cv_ref[...] = jnp.zeros_like(cv_ref)
        cla_ref[...] = jnp.zeros_like(cla_ref)

    lane = lax.broadcasted_iota(jnp.int32, (L, LANES), 1)
    lo_half = lane < B_HEAD_DIM
    eye_packed = ((lane & (B_HEAD_DIM - 1)) == lax.broadcasted_iota(jnp.int32, (L, LANES), 0)).astype(F32)
    lo_half2 = (lax.broadcasted_iota(jnp.int32, (L, 2 * LANES), 1) & B_HEAD_DIM) == 0
    ti = lax.broadcasted_iota(jnp.int32, (L, L), 0)
    si = lax.broadcasted_iota(jnp.int32, (L, L), 1)
    tri = (si <= ti).astype(BF16)
    tri3 = jnp.concatenate([tri, tri, tri], axis=1)
    ri = lax.broadcasted_iota(jnp.int32, (LANES, LANES), 0)
    ci = lax.broadcasted_iota(jnp.int32, (LANES, LANES), 1)
    same_head = (ri < B_HEAD_DIM) == (ci < B_HEAD_DIM)
    seg_ones = jnp.concatenate([same_head.astype(BF16)] * 2, axis=0)
    inv_n = 1.0 / B_HEAD_DIM
    mi = lax.broadcasted_iota(jnp.int32, (2 * L, 4 * L), 0)
    mj = lax.broadcasted_iota(jnp.int32, (2 * L, 4 * L), 1)
    tt, ss = mi & (L - 1), mj & (L - 1)
    tri_mask = (ss < tt) | ((mi >= L) & (ss == tt))
    w_hi, w_lo = _split2(w2a2_ref[...])
    w_hhl = jnp.concatenate([w_hi, w_hi, w_lo], axis=0)

    def stack_heads(x2):
        lo = lo_half if x2.shape[1] == LANES else lo_half2
        xb = x2.astype(BF16)
        return jnp.concatenate([jnp.where(lo, xb, 0.0), jnp.where(lo, 0.0, xb)], axis=0).astype(BF16)

    tb = la_ref.shape[0]
    la_all = la_ref[...]
    la_all = la_all + mula_ref[...] * (_shift_rows(la_all, cla_ref[0:1, :]) - la_all)
    is_wd = lax.broadcasted_iota(jnp.int32, (tb, LANES), 1) < LORA
    x_hi, x_lo = _split2(jnp.concatenate([jnp.where(is_wd, jnp.tanh(la_all), 0.0),
                                          jnp.where(is_wd, 0.0, la_all)], axis=0))
    lora_up = jnp.dot(jnp.concatenate([x_hi, x_lo, x_hi], axis=1), w_hhl, preferred_element_type=F32)

    prev_row = [cr_ref[0:1, :], ck_ref[0:1, :], cv_ref[0:1, :]]

    def prep(c, out):
        rows = slice(c * L, (c + 1) * L)

        def lerp(x_ref, idx, mu_ref):
            x = x_ref[rows, :]
            prev = _shift_rows(x, prev_row[idx])
            prev_row[idx] = x[L - 1:L, :]
            return x + mu_ref[...] * (prev - x)

        r = lerp(pr_ref, 0, mur_ref)
        k = lerp(pk_ref, 1, muk_ref)
        v = lerp(pv_ref, 2, muv_ref)
        lw = -DECAY_SCALE * jax.nn.sigmoid(w0_ref[...] + lora_up[rows, :])
        a_sig = jax.nn.sigmoid(a0_ref[...] + lora_up[tb + c * L:tb + (c + 1) * L, :])
        yield
        cum = jnp.dot(tri3, jnp.concatenate(_split3(lw), axis=0), preferred_element_type=F32)
        e_pos = jnp.exp(cum)
        e_neg = jnp.exp(-cum)
        e_prev = jnp.exp(cum - lw)
        yield
        for p, sl in enumerate(pair_lanes):
            kk_raw = k[:, sl] * kk_ref[:, sl]
            kmod = k[:, sl] * (1.0 + (a_sig[:, sl] - 1.0) * ka_ref[:, sl])
            sums = _seg_sum(jnp.concatenate([kk_raw * kk_raw, r[:, sl] * kmod * rk_ref[:, sl]], axis=0), seg_ones)
            kk = kk_raw * lax.rsqrt(jnp.maximum(sums[:L], 1e-24))
            bt = (kk * a_sig[:, sl]) * e_neg[:, sl]
            kt = kmod * e_neg[:, sl]
            out.append(dict(
                at=(-kk) * e_prev[:, sl], rt=r[:, sl] * e_pos[:, sl], bt=bt, kt=kt,
                bks=jnp.concatenate([stack_heads(bt), stack_heads(kt)], axis=0),
                vb=v[:, sl].astype(BF16), vs=stack_heads(v[:, sl]),
                bonus=sums[L:] * v[:, sl], decay=e_pos[L - 1:L, sl]))
            if p % 2 == 1:
                yield

    def scores(ops):
        for d in ops:
            d["sm"] = jnp.where(tri_mask, _bdot_nt(jnp.concatenate([d["at"], d["rt"]], axis=0), d["bks"]),
                                0.0).astype(BF16)
        for d in ops:
            d["av"] = jnp.dot(d["sm"][:, 2 * L:], d["vs"], preferred_element_type=F32)
            d["x"] = jnp.concatenate([d["at"], d["av"][:L]], axis=1)
            d["pw"] = d["sm"][:L, :2 * L]

    def post(c, ops, o2):
        rows = slice(c * L, (c + 1) * L)
        for p, sl in enumerate(pair_lanes):
            mean = _seg_sum(o2[p], seg_ones) * inv_n
            dev = o2[p] - mean
            var = _seg_sum(dev * dev, seg_ones) * inv_n
            on = dev * lax.rsqrt(var + GN_EPS) * lnw_ref[:, sl] + lnb_ref[:, sl]
            o_ref[rows, sl] = ((on + ops[p]["bonus"]) * _silu(pz_ref[rows, sl])).astype(o_ref.dtype)
            if p % 2 == 1:
                yield

    st = [st_ref[p] for p in range(n_pairs)]
    ops = [[] for _ in range(n_chunks)]
    _drain(prep(0, ops[0]))
    scores(ops[0])
    side = []
    for c in range(n_chunks):
        cur = ops[c]
        if c + 1 < n_chunks:
            side.append(prep(c + 1, ops[c + 1]))
        def spread():
            for gen in side:
                next(gen, None)

        for d in cur:
            d["t"] = eye_packed + d["pw"].astype(F32)
            d["pw"] = jnp.dot(d["pw"], stack_heads(d["pw"]), preferred_element_type=F32).astype(BF16)
        spread()
        n = 2
        while 2 * n < L:
            for d in cur:
                both = jnp.dot(jnp.concatenate([d["pw"], d["t"].astype(BF16)], axis=0), stack_heads(d["pw"]),
                               preferred_element_type=F32)
                d["t"] = d["t"] + both[L:]
                d["pw"] = both[:L].astype(BF16)
            n *= 2
            spread()
        for d in cur:
            d["t"] = d["t"] + jnp.dot(d["t"].astype(BF16), stack_heads(d["pw"]), preferred_element_type=F32)
        spread()
        for d in cur:
            d["x"] = jnp.dot(d["t"].astype(BF16), stack_heads(d["x"]), preferred_element_type=F32)
        for gen in side:
            _drain(gen)
        side = []
        if c + 1 < n_chunks:
            scores(ops[c + 1])
        hs = [_bdot_nt(jnp.concatenate([d["x"][:, :LANES], d["rt"]], axis=0), st[p]) for p, d in enumerate(cur)]
        us = [hs[p][:L] + d["x"][:, LANES:] for p, d in enumerate(cur)]
        o2 = [hs[p][L:] + jnp.dot(d["sm"][L:, :2 * L], stack_heads(us[p]), preferred_element_type=F32)
              + d["av"][L:] for p, d in enumerate(cur)]
        for p, d in enumerate(cur):
            upd = _bdot_tn(jnp.concatenate([us[p].astype(BF16), d["vb"]], axis=0),
                           jnp.concatenate([d["bt"], d["kt"]], axis=0))
            st[p] = (st[p] + jnp.where(same_head, upd, 0.0)) * d["decay"]
        side.append(post(c, cur, o2))
    for gen in side:
        _drain(gen)

    for p in range(n_pairs):
        st_ref[p] = st[p]
    cr_ref[0:1, :], ck_ref[0:1, :], cv_ref[0:1, :] = prev_row
    cla_ref[0:1, :] = la_ref[tb - 1:tb, :]


def _rwkv(proj3, lora3, mu_rkv, mu_la, w0, a0, w2a2, k_k, k_a, r_k, ln_w, ln_b, tokens_per_step):
    bsz, seq, _ = proj3.shape
    tb = tokens_per_step

    def col(group):
        return pl.BlockSpec((None, tb, B_WIDTH), lambda b, c: (b, c, group))

    def vec(n):
        return pl.BlockSpec((1, n), lambda b, c: (0, 0))

    row = lambda t: t.reshape(1, -1)
    return pl.pallas_call(
        _rwkv_kernel,
        grid=(bsz, seq // tb),
        in_specs=[
            col(COL_BR), col(COL_BK), col(COL_BV), col(COL_BZ),
            pl.BlockSpec((None, tb, 2 * LORA), lambda b, c: (b, c, 0)),
            vec(B_WIDTH), vec(B_WIDTH), vec(B_WIDTH), vec(2 * LORA), vec(B_WIDTH), vec(B_WIDTH),
            pl.BlockSpec((2 * LORA, B_WIDTH), lambda b, c: (0, 0)),
            vec(B_WIDTH), vec(B_WIDTH), vec(B_WIDTH), vec(B_WIDTH), vec(B_WIDTH),
        ],
        out_specs=pl.BlockSpec((None, tb, B_WIDTH), lambda b, c: (b, c, 0)),
        out_shape=jax.ShapeDtypeStruct((bsz, seq, B_WIDTH), BF16),
        scratch_shapes=[
            pltpu.VMEM((B_HEADS // 2, LANES, LANES), F32),
            pltpu.VMEM((8, B_WIDTH), F32), pltpu.VMEM((8, B_WIDTH), F32), pltpu.VMEM((8, B_WIDTH), F32),
            pltpu.VMEM((8, 2 * LORA), F32),
        ],
        compiler_params=pltpu.CompilerParams(
            dimension_semantics=("parallel", "arbitrary"), vmem_limit_bytes=VMEM_LIMIT),
        name="rwkv",
    )(proj3, proj3, proj3, proj3, lora3,
      row(mu_rkv[0]), row(mu_rkv[1]), row(mu_rkv[2]), row(mu_la), row(w0), row(a0), w2a2,
      row(k_k), row(k_a), row(r_k), row(ln_w), row(ln_b))


def _merge_out_kernel(ya_ref, yb_ref, yc_ref, ga_ref, gb_ref, gc_ref, wa_ref, wb_ref, wc_ref, wo_ref, x_ref,
                      o_ref):
    merged = (jax.nn.sigmoid(ga_ref[...]) * jnp.dot(ya_ref[...], wa_ref[...], preferred_element_type=F32)
              + jax.nn.sigmoid(gb_ref[...]) * jnp.dot(yb_ref[...], wb_ref[...], preferred_element_type=F32)
              + jax.nn.sigmoid(gc_ref[...]) * jnp.dot(yc_ref[...], wc_ref[...], preferred_element_type=F32))
    o_ref[...] = x_ref[...] + jnp.dot(merged.astype(BF16), wo_ref[...], preferred_element_type=F32)


def _merge_out(ya, yb, yc, proj2, wa, wb, wc, wo, x2d, tm):
    m = x2d.shape[0]

    def ybranch(width):
        return pl.BlockSpec((tm, width), lambda i: (i, 0))

    def gate(group):
        return pl.BlockSpec((tm, D_MODEL), lambda i: (i, group * 1024 // D_MODEL))

    def resident(shape):
        return pl.BlockSpec(shape, lambda i: (0, 0), pipeline_mode=pl.Buffered(1))

    return pl.pallas_call(
        _merge_out_kernel,
        grid=(m // tm,),
        in_specs=[
            ybranch(A_WIDTH), ybranch(B_WIDTH), ybranch(C_WIDTH),
            gate(COL_GA), gate(COL_GB), gate(COL_GC),
            resident((A_WIDTH, D_MODEL)), resident((B_WIDTH, D_MODEL)), resident((C_WIDTH, D_MODEL)),
            resident((D_MODEL, D_MODEL)),
            pl.BlockSpec((tm, D_MODEL), lambda i: (i, 0)),
        ],
        out_specs=pl.BlockSpec((tm, D_MODEL), lambda i: (i, 0)),
        out_shape=jax.ShapeDtypeStruct((m, D_MODEL), F32),
        compiler_params=pltpu.CompilerParams(
            dimension_semantics=("parallel",), vmem_limit_bytes=VMEM_LIMIT),
        name="merge_out",
    )(ya, yb, yc, proj2, proj2, proj2, wa, wb, wc, wo, x2d)


def _layer(x, mem, norm_g, w_in, a_q_g, a_k_g, a_rel_bias, w_up_a,
           b_mu_rkv, b_mu_w, b_mu_a, b_w0, b_w2, b_a0, b_a2, b_k_k, b_k_a, b_r_k,
           b_ln_w, b_ln_b, w_up_b, mem_norm_g, w_mem_kv, c_q_g, c_k_g, w_up_c, w_o):
    bsz, seq, d = x.shape
    t = bsz * seq
    x2d = x.reshape(t, d)
    w_main, w_lora = _prep_w_in(w_in, 128)
    wa, wb, wc, wo, wm = _cast_weights([w_up_a, w_up_b, w_up_c, w_o, w_mem_kv], 8)

    proj, lora = _in_proj(x2d, norm_g, w_main, w_lora, 1024, 1024)
    mkv = _norm_matmul(mem.reshape(bsz * N_MEM, d), mem_norm_g, wm, 1024, 1024, "mem_kv")

    proj3 = proj.reshape(bsz, seq, MAIN_COLS)
    ya = _band_attn(proj3, _band_bias_table(a_rel_bias), a_q_g, a_k_g)
    yc = _mem_attn(proj3, mkv.reshape(bsz, N_MEM, 2 * C_WIDTH), c_q_g, c_k_g, 512)
    yb = _rwkv(proj3, lora.reshape(bsz, seq, 2 * LORA), b_mu_rkv,
               jnp.concatenate([b_mu_w, b_mu_a]), b_w0, b_a0,
               jnp.concatenate([b_w2, b_a2], axis=0), b_k_k, b_k_a, b_r_k.reshape(-1), b_ln_w, b_ln_b, 8 * CHUNK)

    out = _merge_out(ya.reshape(t, A_WIDTH), yb.reshape(t, B_WIDTH), yc.reshape(t, C_WIDTH), proj,
                     wa, wb, wc, wo, x2d, 256)
    return out.reshape(bsz, seq, d)


def kernel(x, mem, norm_g, w_in, a_q_g, a_k_g, a_rel_bias, w_up_a, b_mu_rkv, b_mu_w, b_mu_a, b_w0, b_w2, b_a0, b_a2, b_k_k, b_k_a, b_r_k, b_ln_w, b_ln_b, w_up_b, mem_norm_g, w_mem_kv, c_q_g, c_k_g, w_up_c, w_o):
    for l in range(norm_g.shape[0]):
        x = _layer(x, mem, norm_g[l], w_in[l], a_q_g[l], a_k_g[l], a_rel_bias[l], w_up_a[l],
                   b_mu_rkv[l], b_mu_w[l], b_mu_a[l], b_w0[l], b_w2[l], b_a0[l], b_a2[l],
                   b_k_k[l], b_k_a[l], b_r_k[l], b_ln_w[l], b_ln_b[l], w_up_b[l],
                   mem_norm_g[l], w_mem_kv[l], c_q_g[l], c_k_g[l], w_up_c[l], w_o[l])
    return x
```

```python
import numpy as np
import jax
import jax.numpy as jnp
from jax import lax
from jax.experimental import pallas as pl
from jax.experimental.pallas import tpu as pltpu

D_MODEL = 2048
CHUNK = 64
N_MEM = 256
NORM_EPS = 1e-6
NEG_INF = -1e30

A_HEADS = 8
A_HEAD_DIM = 128
A_WIDTH = A_HEADS * A_HEAD_DIM
A_LEFT_CHUNKS = 8
REL_CLIP = 128

B_HEADS = 16
B_HEAD_DIM = 64
B_WIDTH = B_HEADS * B_HEAD_DIM
LORA = 64
GN_EPS = 64e-5
DECAY_SCALE = float(np.exp(-0.5))

C_HEADS = 4
C_HEAD_DIM = 256
C_WIDTH = C_HEADS * C_HEAD_DIM

LANES = 128
VMEM_LIMIT = 56 * 1024 * 1024

A_BAND = (A_LEFT_CHUNKS + 1) * CHUNK
K_WINDOW = -(-A_BAND // LANES) * LANES
K_SLACK = K_WINDOW - A_BAND
K_PAD = K_WINDOW - CHUNK
ATTN_CHUNKS_IN_FLIGHT = 16
NORM_SUB_ROWS = 256

COL_AQ, COL_AK, COL_AV, COL_AZ = 0, 1, 2, 3
COL_BR, COL_BK, COL_BV, COL_BZ = 4, 5, 6, 7
COL_CQ, COL_CZ = 8, 9
COL_GA, COL_GB, COL_GC = 10, 12, 14
MAIN_COLS = 16 * 1024
LORA_LO = 4 * A_WIDTH + 4 * B_WIDTH
LORA_HI = LORA_LO + 2 * LORA

BF16 = jnp.bfloat16
F32 = jnp.float32


def _bdot(a, b):
    return jnp.dot(a.astype(BF16), b.astype(BF16), preferred_element_type=F32)


def _bdot_nt(a, b):
    return lax.dot_general(a.astype(BF16), b.astype(BF16), (((1,), (1,)), ((), ())),
                           preferred_element_type=F32)


def _bdot_tn(a, b):
    return lax.dot_general(a.astype(BF16), b.astype(BF16), (((0,), (0,)), ((), ())),
                           preferred_element_type=F32)


def _split2(x):
    hi = x.astype(BF16)
    lo = (x - hi.astype(F32)).astype(BF16)
    return hi, lo


def _split3(x):
    hi = x.astype(BF16)
    r1 = x - hi.astype(F32)
    mid = r1.astype(BF16)
    lo = (r1 - mid.astype(F32)).astype(BF16)
    return hi, mid, lo


def _silu(x):
    return x * jax.nn.sigmoid(x)


def _prep_w_in_kernel(w_ref, main_ref, lora_ref):
    main_ref[:, :LORA_LO] = w_ref[:, :LORA_LO].astype(BF16)
    main_ref[:, LORA_LO:] = w_ref[:, LORA_HI:].astype(BF16)
    lora_ref[...] = w_ref[:, LORA_LO:LORA_HI].astype(BF16)


def _prep_w_in(w_in, rows):
    k, n = w_in.shape
    return pl.pallas_call(
        _prep_w_in_kernel,
        grid=(k // rows,),
        in_specs=[pl.BlockSpec((rows, n), lambda i: (i, 0))],
        out_specs=[
            pl.BlockSpec((rows, MAIN_COLS), lambda i: (i, 0)),
            pl.BlockSpec((rows, 2 * LORA), lambda i: (i, 0)),
        ],
        out_shape=[
            jax.ShapeDtypeStruct((k, MAIN_COLS), BF16),
            jax.ShapeDtypeStruct((k, 2 * LORA), BF16),
        ],
        compiler_params=pltpu.CompilerParams(
            dimension_semantics=("parallel",), vmem_limit_bytes=VMEM_LIMIT),
        name="prep_w_in",
    )(w_in)


def _cast_weights_kernel(*refs):
    n = len(refs) // 2
    for src, dst in zip(refs[:n], refs[n:]):
        dst[...] = src[...].astype(BF16)


def _cast_weights(ws, n_steps):
    specs = [pl.BlockSpec((w.shape[0] // n_steps, w.shape[1]), lambda i: (i, 0)) for w in ws]
    return pl.pallas_call(
        _cast_weights_kernel,
        grid=(n_steps,),
        in_specs=specs,
        out_specs=specs,
        out_shape=[jax.ShapeDtypeStruct(w.shape, BF16) for w in ws],
        compiler_params=pltpu.CompilerParams(
            dimension_semantics=("parallel",), vmem_limit_bytes=VMEM_LIMIT),
        name="cast_weights",
    )(*ws)


def _norm_matmul_kernel(x_ref, g_ref, w_ref, o_ref, h_ref):
    @pl.when(pl.program_id(1) == 0)
    def _():
        x = x_ref[...]
        ms = jnp.mean(x * x, axis=-1, keepdims=True)
        h_ref[...] = (x * lax.rsqrt(ms + NORM_EPS) * g_ref[...]).astype(BF16)

    o_ref[...] = jnp.dot(h_ref[...], w_ref[...], preferred_element_type=F32).astype(o_ref.dtype)


def _norm_matmul(x2d, g, w_bf16, tm, tn, name):
    m, k = x2d.shape
    n = w_bf16.shape[1]
    return pl.pallas_call(
        _norm_matmul_kernel,
        grid=(m // tm, n // tn),
        in_specs=[
            pl.BlockSpec((tm, k), lambda i, j: (i, 0)),
            pl.BlockSpec((1, k), lambda i, j: (0, 0)),
            pl.BlockSpec((k, tn), lambda i, j: (0, j)),
        ],
        out_specs=pl.BlockSpec((tm, tn), lambda i, j: (i, j)),
        out_shape=jax.ShapeDtypeStruct((m, n), F32),
        scratch_shapes=[pltpu.VMEM((tm, k), BF16)],
        compiler_params=pltpu.CompilerParams(
            dimension_semantics=("parallel", "arbitrary"), vmem_limit_bytes=VMEM_LIMIT),
        name=name,
    )(x2d, g.reshape(1, k), w_bf16)


def _in_proj_kernel(x_ref, g_ref, w_ref, wla_ref, o_ref, la_ref, h_ref):
    j = pl.program_id(1)

    @pl.when(j == 0)
    def _():
        for r0 in range(0, x_ref.shape[0], NORM_SUB_ROWS):
            rows = slice(r0, r0 + NORM_SUB_ROWS)
            x = x_ref[rows, :]
            ms = jnp.mean(x * x, axis=-1, keepdims=True)
            h = (x * lax.rsqrt(ms + NORM_EPS) * g_ref[...]).astype(BF16)
            h_ref[rows, :] = h
            la_ref[rows, :] = jnp.dot(h, wla_ref[...], preferred_element_type=F32)
            o_ref[rows, :] = jnp.dot(h, w_ref[...], preferred_element_type=F32)

    @pl.when(j != 0)
    def _():
        o_ref[...] = jnp.dot(h_ref[...], w_ref[...], preferred_element_type=F32)


def _in_proj(x2d, g, w_main, w_lora, tm, tn):
    m, k = x2d.shape
    n = w_main.shape[1]
    assert m % tm == 0 and n % tn == 0 and tm % NORM_SUB_ROWS == 0
    return pl.pallas_call(
        _in_proj_kernel,
        grid=(m // tm, n // tn),
        in_specs=[
            pl.BlockSpec((tm, k), lambda i, j: (i, 0)),
            pl.BlockSpec((1, k), lambda i, j: (0, 0)),
            pl.BlockSpec((k, tn), lambda i, j: (0, j)),
            pl.BlockSpec((k, 2 * LORA), lambda i, j: (0, 0)),
        ],
        out_specs=[
            pl.BlockSpec((tm, tn), lambda i, j: (i, j)),
            pl.BlockSpec((tm, 2 * LORA), lambda i, j: (i, 0)),
        ],
        out_shape=[
            jax.ShapeDtypeStruct((m, n), F32),
            jax.ShapeDtypeStruct((m, 2 * LORA), F32),
        ],
        scratch_shapes=[pltpu.VMEM((tm, k), BF16)],
        compiler_params=pltpu.CompilerParams(
            dimension_semantics=("parallel", "arbitrary"), vmem_limit_bytes=VMEM_LIMIT),
        name="in_proj",
    )(x2d, g.reshape(1, k), w_main, w_lora)


def _band_attn_kernel(q_ref, k_ref, v_ref, z_ref, bias_ref, gq_ref, gk_ref, o_ref,
                      qn_ref, kp_ref, vp_ref):
    seq = q_ref.shape[0]
    scale = A_HEAD_DIM ** -0.5
    q = q_ref[...]
    qn_ref[...] = (q * lax.rsqrt(jnp.mean(q * q, axis=-1, keepdims=True) + NORM_EPS)).astype(BF16)
    k = k_ref[...]
    kn = k * lax.rsqrt(jnp.mean(k * k, axis=-1, keepdims=True) + NORM_EPS) * (gk_ref[...] * gq_ref[...] * scale)
    kp_ref[0:K_PAD, :] = jnp.zeros((K_PAD, A_HEAD_DIM), BF16)
    vp_ref[0:K_PAD, :] = jnp.zeros((K_PAD, A_HEAD_DIM), BF16)
    kp_ref[K_PAD:, :] = kn.astype(BF16)
    vp_ref[K_PAD:, :] = v_ref[...].astype(BF16)
    def scores(c):
        q0 = c * CHUNK
        first = K_PAD - q0
        skip = max(first, 0) // LANES * LANES
        s = _bdot_nt(qn_ref[q0:q0 + CHUNK, :], kp_ref[q0 + skip:q0 + K_WINDOW, :]) + bias_ref[:, skip:]
        if first > skip:
            col = lax.broadcasted_iota(jnp.int32, s.shape, 1)
            s = jnp.where(col >= first - skip, s, NEG_INF)
        return s, skip

    for c0 in range(0, seq // CHUNK, ATTN_CHUNKS_IN_FLIGHT):
        group = range(c0, c0 + ATTN_CHUNKS_IN_FLIGHT)
        sc = [scores(c) for c in group]
        mx = [jnp.max(s, axis=-1, keepdims=True) for s, _ in sc]
        ps = [jnp.exp(s - m) for (s, _), m in zip(sc, mx)]
        ls = [jnp.sum(p, axis=-1, keepdims=True) for p in ps]
        outs = [_bdot(p, vp_ref[c * CHUNK + skip:c * CHUNK + K_WINDOW, :]) / l
                for c, p, l, (_, skip) in zip(group, ps, ls, sc)]
        for c, o in zip(group, outs):
            q0 = c * CHUNK
            o_ref[q0:q0 + CHUNK, :] = (o * _silu(z_ref[q0:q0 + CHUNK, :])).astype(o_ref.dtype)


def _band_bias_table(rel_bias):
    period = 1024
    q = np.arange(period)
    m_minus_r = np.where(q < K_WINDOW, q, q - period)
    dist = K_PAD - m_minus_r
    f = rel_bias[:, np.clip(dist, -REL_CLIP, REL_CLIP) + REL_CLIP].astype(F32)
    seq = jnp.tile(f, (1, CHUNK))[:, :CHUNK * (period - 1)]
    toeplitz = seq.reshape(-1, CHUNK, period - 1)[:, :, :K_WINDOW]
    in_band = np.arange(K_WINDOW)[None, None, :] >= K_SLACK
    return jnp.where(jnp.asarray(in_band), toeplitz, NEG_INF)


def _band_attn(proj3, bias_tab, gq, gk):
    bsz, seq, _ = proj3.shape
    hb = A_WIDTH // A_HEAD_DIM
    assert seq % (CHUNK * ATTN_CHUNKS_IN_FLIGHT) == 0

    def col(group):
        return pl.BlockSpec((None, seq, A_HEAD_DIM), lambda b, h: (b, 0, group * hb + h))

    return pl.pallas_call(
        _band_attn_kernel,
        grid=(bsz, A_HEADS),
        in_specs=[
            col(COL_AQ), col(COL_AK), col(COL_AV), col(COL_AZ),
            pl.BlockSpec((None, CHUNK, K_WINDOW), lambda b, h: (h, 0, 0)),
            pl.BlockSpec((1, A_HEAD_DIM), lambda b, h: (0, 0)),
            pl.BlockSpec((1, A_HEAD_DIM), lambda b, h: (0, 0)),
        ],
        out_specs=pl.BlockSpec((None, seq, A_HEAD_DIM), lambda b, h: (b, 0, h)),
        out_shape=jax.ShapeDtypeStruct((bsz, seq, A_WIDTH), BF16),
        scratch_shapes=[
            pltpu.VMEM((seq, A_HEAD_DIM), BF16),
            pltpu.VMEM((seq + K_PAD, A_HEAD_DIM), BF16),
            pltpu.VMEM((seq + K_PAD, A_HEAD_DIM), BF16),
        ],
        compiler_params=pltpu.CompilerParams(
            dimension_semantics=("parallel", "parallel"), vmem_limit_bytes=VMEM_LIMIT),
        name="band_attn",
    )(proj3, proj3, proj3, proj3, bias_tab, gq.reshape(1, -1), gk.reshape(1, -1))


def _mem_attn_kernel(q_ref, z_ref, mk_ref, mv_ref, gq_ref, gk_ref, o_ref, kn_ref):
    scale = C_HEAD_DIM ** -0.5
    heads = [slice(h * C_HEAD_DIM, (h + 1) * C_HEAD_DIM) for h in range(C_HEADS)]

    def unit_rms(x):
        return x * lax.rsqrt(jnp.mean(x * x, axis=-1, keepdims=True) + NORM_EPS)

    @pl.when(pl.program_id(1) == 0)
    def _():
        gain = gk_ref[...] * gq_ref[...] * scale
        for sl in heads:
            kn_ref[:, sl] = (unit_rms(mk_ref[:, sl]) * gain).astype(BF16)

    qn = [unit_rms(q_ref[:, sl]) for sl in heads]
    kn = [kn_ref[:, sl] for sl in heads]
    sc = [_bdot_nt(q, k) for q, k in zip(qn, kn)]
    mx = [jnp.max(s, axis=-1, keepdims=True) for s in sc]
    ps = [jnp.exp(s - m) for s, m in zip(sc, mx)]
    ls = [jnp.sum(p, axis=-1, keepdims=True) for p in ps]
    outs = [_bdot(p, mv_ref[:, sl]) / l for p, l, sl in zip(ps, ls, heads)]
    for o, sl in zip(outs, heads):
        o_ref[:, sl] = (o * _silu(z_ref[:, sl])).astype(o_ref.dtype)


def _mem_attn(proj3, mkv3, gq, gk, ts):
    bsz, seq, _ = proj3.shape
    return pl.pallas_call(
        _mem_attn_kernel,
        grid=(bsz, seq // ts),
        in_specs=[
            pl.BlockSpec((None, ts, C_WIDTH), lambda b, s: (b, s, COL_CQ)),
            pl.BlockSpec((None, ts, C_WIDTH), lambda b, s: (b, s, COL_CZ)),
            pl.BlockSpec((None, N_MEM, C_WIDTH), lambda b, s: (b, 0, 0)),
            pl.BlockSpec((None, N_MEM, C_WIDTH), lambda b, s: (b, 0, 1)),
            pl.BlockSpec((1, C_HEAD_DIM), lambda b, s: (0, 0)),
            pl.BlockSpec((1, C_HEAD_DIM), lambda b, s: (0, 0)),
        ],
        out_specs=pl.BlockSpec((None, ts, C_WIDTH), lambda b, s: (b, s, 0)),
        out_shape=jax.ShapeDtypeStruct((bsz, seq, C_WIDTH), BF16),
        scratch_shapes=[pltpu.VMEM((N_MEM, C_WIDTH), BF16)],
        compiler_params=pltpu.CompilerParams(
            dimension_semantics=("parallel", "arbitrary"), vmem_limit_bytes=VMEM_LIMIT),
        name="mem_attn",
    )(proj3, proj3, mkv3, mkv3, gq.reshape(1, -1), gk.reshape(1, -1))


def _shift_rows(x, carry_row):
    rolled = pltpu.roll(x, 1, 0)
    row = lax.broadcasted_iota(jnp.int32, x.shape, 0)
    return jnp.where(row == 0, carry_row, rolled)


def _seg_sum(x, seg_ones2):
    hi, lo = _split2(x)
    return jnp.dot(jnp.concatenate([hi, lo], axis=1), seg_ones2, preferred_element_type=F32)


def _drain(gen):
    for _ in gen:
        pass


def _rwkv_kernel(pr_ref, pk_ref, pv_ref, pz_ref, la_ref,
                 mur_ref, muk_ref, muv_ref, mula_ref, w0_ref, a0_ref, w2a2_ref,
                 kk_ref, ka_ref, rk_ref, lnw_ref, lnb_ref,
                 o_ref,
                 st_ref, cr_ref, ck_ref, cv_ref, cla_ref):
    L = CHUNK
    nb, tb = pr_ref.shape[0], pr_ref.shape[1]
    n_chunks = tb // L
    n_pairs = B_HEADS // 2
    pair_lanes = [slice(p * LANES, (p + 1) * LANES) for p in range(n_pairs)]

    @pl.when(pl.program_id(1) == 0)
    def _():
        st_ref[...] = jnp.zeros_like(st_ref)
        cr_ref[...] = jnp.zeros_like(cr_ref)
        ck_ref[...] = jnp.zeros_like(ck_ref)
        cv_ref[...] = jnp.zeros_like(cv_ref)
        cla_ref[...] = jnp.zeros_like(cla_ref)

    lane = lax.broadcasted_iota(jnp.int32, (L, LANES), 1)
    lo_half = lane < B_HEAD_DIM
    eye_packed = ((lane & (B_HEAD_DIM - 1)) == lax.broadcasted_iota(jnp.int32, (L, LANES), 0)).astype(F32)
    lo_half2 = (lax.broadcasted_iota(jnp.int32, (L, 2 * LANES), 1) & B_HEAD_DIM) == 0
    ti = lax.broadcasted_iota(jnp.int32, (L, L), 0)
    si = lax.broadcasted_iota(jnp.int32, (L, L), 1)
    tri = (si <= ti).astype(BF16)
    tri3 = jnp.concatenate([tri, tri, tri], axis=1)
    ri = lax.broadcasted_iota(jnp.int32, (LANES, LANES), 0)
    ci = lax.broadcasted_iota(jnp.int32, (LANES, LANES), 1)
    same_head = (ri < B_HEAD_DIM) == (ci < B_HEAD_DIM)
    seg_ones = jnp.concatenate([same_head.astype(BF16)] * 2, axis=0)
    inv_n = 1.0 / B_HEAD_DIM
    mi = lax.broadcasted_iota(jnp.int32, (2 * L, 4 * L), 0)
    mj = lax.broadcasted_iota(jnp.int32, (2 * L, 4 * L), 1)
    tt, ss = mi & (L - 1), mj & (L - 1)
    tri_mask = (ss < tt) | ((mi >= L) & (ss == tt))
    w_hi, w_lo = _split2(w2a2_ref[...])
    w_hhl = jnp.concatenate([w_hi, w_hi, w_lo], axis=0)

    def stack_heads(x2):
        lo = lo_half if x2.shape[1] == LANES else lo_half2
        xb = x2.astype(BF16)
        return jnp.concatenate([jnp.where(lo, xb, 0.0), jnp.where(lo, 0.0, xb)], axis=0).astype(BF16)

    la_all = [la_ref[bi] for bi in range(nb)]
    la_all = [x + mula_ref[...] * (_shift_rows(x, cla_ref[bi:bi + 1, :]) - x)
              for bi, x in enumerate(la_all)]
    is_wd = lax.broadcasted_iota(jnp.int32, (tb, LANES), 1) < LORA
    x_hi, x_lo = _split2(jnp.concatenate([jnp.where(is_wd, jnp.tanh(x), 0.0) for x in la_all]
                                         + [jnp.where(is_wd, 0.0, x) for x in la_all], axis=0))
    lora_up = jnp.dot(jnp.concatenate([x_hi, x_lo, x_hi], axis=1), w_hhl, preferred_element_type=F32)

    prev_row = [[cr_ref[bi:bi + 1, :], ck_ref[bi:bi + 1, :], cv_ref[bi:bi + 1, :]] for bi in range(nb)]

    def prep(c, out):
        rows = slice(c * L, (c + 1) * L)
        for bi in range(nb):
            def lerp(x_ref, idx, mu_ref):
                x = x_ref[bi, rows, :]
                prev = _shift_rows(x, prev_row[bi][idx])
                prev_row[bi][idx] = x[L - 1:L, :]
                return x + mu_ref[...] * (prev - x)

            r = lerp(pr_ref, 0, mur_ref)
            k = lerp(pk_ref, 1, muk_ref)
            v = lerp(pv_ref, 2, muv_ref)
            r0 = bi * tb + c * L
            lw = -DECAY_SCALE * jax.nn.sigmoid(w0_ref[...] + lora_up[r0:r0 + L, :])
            a_sig = jax.nn.sigmoid(a0_ref[...] + lora_up[nb * tb + r0:nb * tb + r0 + L, :])
            yield
            cum = jnp.dot(tri3, jnp.concatenate(_split3(lw), axis=0), preferred_element_type=F32)
            e_pos = jnp.exp(cum)
            e_neg = jnp.exp(-cum)
            e_prev = jnp.exp(cum - lw)
            yield
            for p, sl in enumerate(pair_lanes):
                kk_raw = k[:, sl] * kk_ref[:, sl]
                kmod = k[:, sl] * (1.0 + (a_sig[:, sl] - 1.0) * ka_ref[:, sl])
                sums = _seg_sum(jnp.concatenate([kk_raw * kk_raw, r[:, sl] * kmod * rk_ref[:, sl]], axis=0),
                                seg_ones)
                kk = kk_raw * lax.rsqrt(jnp.maximum(sums[:L], 1e-24))
                bt = (kk * a_sig[:, sl]) * e_neg[:, sl]
                kt = kmod * e_neg[:, sl]
                out.append(dict(
                    bi=bi, sl=sl,
                    at=(-kk) * e_prev[:, sl], rt=r[:, sl] * e_pos[:, sl], bt=bt, kt=kt,
                    bks=jnp.concatenate([stack_heads(bt), stack_heads(kt)], axis=0),
                    vb=v[:, sl].astype(BF16), vs=stack_heads(v[:, sl]),
                    bonus=sums[L:] * v[:, sl], decay=e_pos[L - 1:L, sl]))
                if p % 2 == 1:
                    yield

    def scores(ops):
        for d in ops:
            d["sm"] = jnp.where(tri_mask, _bdot_nt(jnp.concatenate([d["at"], d["rt"]], axis=0), d["bks"]),
                                0.0).astype(BF16)
        for d in ops:
            d["av"] = jnp.dot(d["sm"][:, 2 * L:], d["vs"], preferred_element_type=F32)
            d["x"] = jnp.concatenate([d["at"], d["av"][:L]], axis=1)
            d["pw"] = d["sm"][:L, :2 * L]

    def post(c, ops, o2):
        rows = slice(c * L, (c + 1) * L)
        for u, d in enumerate(ops):
            bi, sl = d["bi"], d["sl"]
            mean = _seg_sum(o2[u], seg_ones) * inv_n
            dev = o2[u] - mean
            var = _seg_sum(dev * dev, seg_ones) * inv_n
            on = dev * lax.rsqrt(var + GN_EPS) * lnw_ref[:, sl] + lnb_ref[:, sl]
            o_ref[bi, rows, sl] = ((on + d["bonus"]) * _silu(pz_ref[bi, rows, sl])).astype(o_ref.dtype)
            if u % 2 == 1:
                yield

    st = [st_ref[u] for u in range(nb * n_pairs)]
    ops = [[] for _ in range(n_chunks)]
    _drain(prep(0, ops[0]))
    scores(ops[0])
    side = []
    for c in range(n_chunks):
        cur = ops[c]
        if c + 1 < n_chunks:
            side.append(prep(c + 1, ops[c + 1]))
        def spread():
            for gen in side:
                for _ in range(nb):
                    next(gen, None)

        for d in cur:
            d["t"] = eye_packed + d["pw"].astype(F32)
            d["pw"] = jnp.dot(d["pw"], stack_heads(d["pw"]), preferred_element_type=F32).astype(BF16)
        spread()
        n = 2
        while 2 * n < L:
            for d in cur:
                both = jnp.dot(jnp.concatenate([d["pw"], d["t"].astype(BF16)], axis=0), stack_heads(d["pw"]),
                               preferred_element_type=F32)
                d["t"] = d["t"] + both[L:]
                d["pw"] = both[:L].astype(BF16)
            n *= 2
            spread()
        for d in cur:
            d["t"] = d["t"] + jnp.dot(d["t"].astype(BF16), stack_heads(d["pw"]), preferred_element_type=F32)
        spread()
        for d in cur:
            d["x"] = jnp.dot(d["t"].astype(BF16), stack_heads(d["x"]), preferred_element_type=F32)
        for gen in side:
            _drain(gen)
        side = []
        if c + 1 < n_chunks:
            scores(ops[c + 1])
        hs = [_bdot_nt(jnp.concatenate([d["x"][:, :LANES], d["rt"]], axis=0), st[p]) for p, d in enumerate(cur)]
        us = [hs[p][:L] + d["x"][:, LANES:] for p, d in enumerate(cur)]
        o2 = [hs[p][L:] + jnp.dot(d["sm"][L:, :2 * L], stack_heads(us[p]), preferred_element_type=F32)
              + d["av"][L:] for p, d in enumerate(cur)]
        for p, d in enumerate(cur):
            upd = _bdot_tn(jnp.concatenate([us[p].astype(BF16), d["vb"]], axis=0),
                           jnp.concatenate([d["bt"], d["kt"]], axis=0))
            st[p] = (st[p] + jnp.where(same_head, upd, 0.0)) * d["decay"]
        side.append(post(c, cur, o2))
    for gen in side:
        _drain(gen)

    for u in range(nb * n_pairs):
        st_ref[u] = st[u]
    for bi in range(nb):
        cr_ref[bi:bi + 1, :], ck_ref[bi:bi + 1, :], cv_ref[bi:bi + 1, :] = prev_row[bi]
        cla_ref[bi:bi + 1, :] = la_ref[bi, tb - 1:tb, :]


def _rwkv(proj3, lora3, mu_rkv, mu_la, w0, a0, w2a2, k_k, k_a, r_k, ln_w, ln_b, tokens_per_step, batch_per_step):
    bsz, seq, _ = proj3.shape
    tb = tokens_per_step
    nb = batch_per_step
    assert nb <= 8 and bsz % nb == 0 and seq % tb == 0 and tb % CHUNK == 0

    def col(group):
        return pl.BlockSpec((nb, tb, B_WIDTH), lambda b, c: (b, c, group))

    def vec(n):
        return pl.BlockSpec((1, n), lambda b, c: (0, 0))

    row = lambda t: t.reshape(1, -1)
    return pl.pallas_call(
        _rwkv_kernel,
        grid=(bsz // nb, seq // tb),
        in_specs=[
            col(COL_BR), col(COL_BK), col(COL_BV), col(COL_BZ),
            pl.BlockSpec((nb, tb, 2 * LORA), lambda b, c: (b, c, 0)),
            vec(B_WIDTH), vec(B_WIDTH), vec(B_WIDTH), vec(2 * LORA), vec(B_WIDTH), vec(B_WIDTH),
            pl.BlockSpec((2 * LORA, B_WIDTH), lambda b, c: (0, 0)),
            vec(B_WIDTH), vec(B_WIDTH), vec(B_WIDTH), vec(B_WIDTH), vec(B_WIDTH),
        ],
        out_specs=pl.BlockSpec((nb, tb, B_WIDTH), lambda b, c: (b, c, 0)),
        out_shape=jax.ShapeDtypeStruct((bsz, seq, B_WIDTH), BF16),
        scratch_shapes=[
            pltpu.VMEM((nb * B_HEADS // 2, LANES, LANES), F32),
            pltpu.VMEM((8, B_WIDTH), F32), pltpu.VMEM((8, B_WIDTH), F32), pltpu.VMEM((8, B_WIDTH), F32),
            pltpu.VMEM((8, 2 * LORA), F32),
        ],
        compiler_params=pltpu.CompilerParams(
            dimension_semantics=("parallel", "arbitrary"), vmem_limit_bytes=VMEM_LIMIT),
        name="rwkv",
    )(proj3, proj3, proj3, proj3, lora3,
      row(mu_rkv[0]), row(mu_rkv[1]), row(mu_rkv[2]), row(mu_la), row(w0), row(a0), w2a2,
      row(k_k), row(k_a), row(r_k), row(ln_w), row(ln_b))


def _merge_out_kernel(ya_ref, yb_ref, yc_ref, ga_ref, gb_ref, gc_ref, wa_ref, wb_ref, wc_ref, wo_ref, x_ref,
                      o_ref):
    merged = (jax.nn.sigmoid(ga_ref[...]) * jnp.dot(ya_ref[...], wa_ref[...], preferred_element_type=F32)
              + jax.nn.sigmoid(gb_ref[...]) * jnp.dot(yb_ref[...], wb_ref[...], preferred_element_type=F32)
              + jax.nn.sigmoid(gc_ref[...]) * jnp.dot(yc_ref[...], wc_ref[...], preferred_element_type=F32))
    o_ref[...] = x_ref[...] + jnp.dot(merged.astype(BF16), wo_ref[...], preferred_element_type=F32)


def _merge_out(ya, yb, yc, proj2, wa, wb, wc, wo, x2d, tm):
    m = x2d.shape[0]

    def ybranch(width):
        return pl.BlockSpec((tm, width), lambda i: (i, 0))

    def gate(group):
        return pl.BlockSpec((tm, D_MODEL), lambda i: (i, group * 1024 // D_MODEL))

    def resident(shape):
        return pl.BlockSpec(shape, lambda i: (0, 0), pipeline_mode=pl.Buffered(1))

    return pl.pallas_call(
        _merge_out_kernel,
        grid=(m // tm,),
        in_specs=[
            ybranch(A_WIDTH), ybranch(B_WIDTH), ybranch(C_WIDTH),
            gate(COL_GA), gate(COL_GB), gate(COL_GC),
            resident((A_WIDTH, D_MODEL)), resident((B_WIDTH, D_MODEL)), resident((C_WIDTH, D_MODEL)),
            resident((D_MODEL, D_MODEL)),
            pl.BlockSpec((tm, D_MODEL), lambda i: (i, 0)),
        ],
        out_specs=pl.BlockSpec((tm, D_MODEL), lambda i: (i, 0)),
        out_shape=jax.ShapeDtypeStruct((m, D_MODEL), F32),
        compiler_params=pltpu.CompilerParams(
            dimension_semantics=("parallel",), vmem_limit_bytes=VMEM_LIMIT),
        name="merge_out",
    )(ya, yb, yc, proj2, proj2, proj2, wa, wb, wc, wo, x2d)


def _layer(x, mem, norm_g, w_in, a_q_g, a_k_g, a_rel_bias, w_up_a,
           b_mu_rkv, b_mu_w, b_mu_a, b_w0, b_w2, b_a0, b_a2, b_k_k, b_k_a, b_r_k,
           b_ln_w, b_ln_b, w_up_b, mem_norm_g, w_mem_kv, c_q_g, c_k_g, w_up_c, w_o):
    bsz, seq, d = x.shape
    t = bsz * seq
    x2d = x.reshape(t, d)
    w_main, w_lora = _prep_w_in(w_in, 128)
    wa, wb, wc, wo, wm = _cast_weights([w_up_a, w_up_b, w_up_c, w_o, w_mem_kv], 8)

    proj, lora = _in_proj(x2d, norm_g, w_main, w_lora, 1024, 1024)
    mkv = _norm_matmul(mem.reshape(bsz * N_MEM, d), mem_norm_g, wm, 1024, 1024, "mem_kv")

    proj3 = proj.reshape(bsz, seq, MAIN_COLS)
    ya = _band_attn(proj3, _band_bias_table(a_rel_bias), a_q_g, a_k_g)
    yc = _mem_attn(proj3, mkv.reshape(bsz, N_MEM, 2 * C_WIDTH), c_q_g, c_k_g, 512)
    yb = _rwkv(proj3, lora.reshape(bsz, seq, 2 * LORA), b_mu_rkv,
               jnp.concatenate([b_mu_w, b_mu_a]), b_w0, b_a0,
               jnp.concatenate([b_w2, b_a2], axis=0), b_k_k, b_k_a, b_r_k.reshape(-1), b_ln_w, b_ln_b, 4 * CHUNK, 2)

    out = _merge_out(ya.reshape(t, A_WIDTH), yb.reshape(t, B_WIDTH), yc.reshape(t, C_WIDTH), proj,
                     wa, wb, wc, wo, x2d, 256)
    return out.reshape(bsz, seq, d)


def kernel(x, mem, norm_g, w_in, a_q_g, a_k_g, a_rel_bias, w_up_a, b_mu_rkv, b_mu_w, b_mu_a, b_w0, b_w2, b_a0, b_a2, b_k_k, b_k_a, b_r_k, b_ln_w, b_ln_b, w_up_b, mem_norm_g, w_mem_kv, c_q_g, c_k_g, w_up_c, w_o):
    for l in range(norm_g.shape[0]):
        x = _layer(x, mem, norm_g[l], w_in[l], a_q_g[l], a_k_g[l], a_rel_bias[l], w_up_a[l],
                   b_mu_rkv[l], b_mu_w[l], b_mu_a[l], b_w0[l], b_w2[l], b_a0[l], b_a2[l],
                   b_k_k[l], b_k_a[l], b_r_k[l], b_ln_w[l], b_ln_b[l], w_up_b[l],
                   mem_norm_g[l], w_mem_kv[l], c_q_g[l], c_k_g[l], w_up_c[l], w_o[l])
    return x
```

```python
import numpy as np
import jax
import jax.numpy as jnp
from jax import lax
from jax.experimental import pallas as pl
from jax.experimental.pallas import tpu as pltpu

D_MODEL = 2048
CHUNK = 64
N_MEM = 256
NORM_EPS = 1e-6
NEG_INF = -1e30

A_HEADS = 8
A_HEAD_DIM = 128
A_WIDTH = A_HEADS * A_HEAD_DIM
A_LEFT_CHUNKS = 8
REL_CLIP = 128

B_HEADS = 16
B_HEAD_DIM = 64
B_WIDTH = B_HEADS * B_HEAD_DIM
LORA = 64
GN_EPS = 64e-5
DECAY_SCALE = float(np.exp(-0.5))

C_HEADS = 4
C_HEAD_DIM = 256
C_WIDTH = C_HEADS * C_HEAD_DIM

LANES = 128
VMEM_LIMIT = 56 * 1024 * 1024

A_BAND = (A_LEFT_CHUNKS + 1) * CHUNK
K_WINDOW = -(-A_BAND // LANES) * LANES
K_SLACK = K_WINDOW - A_BAND
K_PAD = K_WINDOW - CHUNK
ATTN_CHUNKS_IN_FLIGHT = 16
ATTN_HEADS_PER_STEP = 2
NORM_SUB_ROWS = 256

COL_AQ, COL_AK, COL_AV, COL_AZ = 0, 1, 2, 3
COL_BR, COL_BK, COL_BV, COL_BZ = 4, 5, 6, 7
COL_CQ, COL_CZ = 8, 9
COL_GA, COL_GB, COL_GC = 10, 12, 14
MAIN_COLS = 16 * 1024
LORA_LO = 4 * A_WIDTH + 4 * B_WIDTH
LORA_HI = LORA_LO + 2 * LORA

BF16 = jnp.bfloat16
F32 = jnp.float32


def _bdot(a, b):
    return jnp.dot(a.astype(BF16), b.astype(BF16), preferred_element_type=F32)


def _bdot_nt(a, b):
    return lax.dot_general(a.astype(BF16), b.astype(BF16), (((1,), (1,)), ((), ())),
                           preferred_element_type=F32)


def _bdot_tn(a, b):
    return lax.dot_general(a.astype(BF16), b.astype(BF16), (((0,), (0,)), ((), ())),
                           preferred_element_type=F32)


def _split2(x):
    hi = x.astype(BF16)
    lo = (x - hi.astype(F32)).astype(BF16)
    return hi, lo


def _split3(x):
    hi = x.astype(BF16)
    r1 = x - hi.astype(F32)
    mid = r1.astype(BF16)
    lo = (r1 - mid.astype(F32)).astype(BF16)
    return hi, mid, lo


def _silu(x):
    h = 0.5 * x
    return h + h * jnp.tanh(h)


def _prep_w_in_kernel(w_ref, main_ref, lora_ref):
    main_ref[:, :LORA_LO] = w_ref[:, :LORA_LO].astype(BF16)
    main_ref[:, LORA_LO:] = w_ref[:, LORA_HI:].astype(BF16)
    lora_ref[...] = w_ref[:, LORA_LO:LORA_HI].astype(BF16)


def _prep_w_in(w_in, rows):
    k, n = w_in.shape
    return pl.pallas_call(
        _prep_w_in_kernel,
        grid=(k // rows,),
        in_specs=[pl.BlockSpec((rows, n), lambda i: (i, 0))],
        out_specs=[
            pl.BlockSpec((rows, MAIN_COLS), lambda i: (i, 0)),
            pl.BlockSpec((rows, 2 * LORA), lambda i: (i, 0)),
        ],
        out_shape=[
            jax.ShapeDtypeStruct((k, MAIN_COLS), BF16),
            jax.ShapeDtypeStruct((k, 2 * LORA), BF16),
        ],
        compiler_params=pltpu.CompilerParams(
            dimension_semantics=("parallel",), vmem_limit_bytes=VMEM_LIMIT),
        name="prep_w_in",
    )(w_in)


def _cast_weights_kernel(*refs):
    n = len(refs) // 2
    for src, dst in zip(refs[:n], refs[n:]):
        dst[...] = src[...].astype(BF16)


def _cast_weights(ws, n_steps):
    specs = [pl.BlockSpec((w.shape[0] // n_steps, w.shape[1]), lambda i: (i, 0)) for w in ws]
    return pl.pallas_call(
        _cast_weights_kernel,
        grid=(n_steps,),
        in_specs=specs,
        out_specs=specs,
        out_shape=[jax.ShapeDtypeStruct(w.shape, BF16) for w in ws],
        compiler_params=pltpu.CompilerParams(
            dimension_semantics=("parallel",), vmem_limit_bytes=VMEM_LIMIT),
        name="cast_weights",
    )(*ws)


def _norm_matmul_kernel(x_ref, g_ref, w_ref, o_ref, h_ref):
    @pl.when(pl.program_id(1) == 0)
    def _():
        x = x_ref[...]
        ms = jnp.mean(x * x, axis=-1, keepdims=True)
        h_ref[...] = (x * lax.rsqrt(ms + NORM_EPS) * g_ref[...]).astype(BF16)

    o_ref[...] = jnp.dot(h_ref[...], w_ref[...], preferred_element_type=F32).astype(o_ref.dtype)


def _norm_matmul(x2d, g, w_bf16, tm, tn, name):
    m, k = x2d.shape
    n = w_bf16.shape[1]
    return pl.pallas_call(
        _norm_matmul_kernel,
        grid=(m // tm, n // tn),
        in_specs=[
            pl.BlockSpec((tm, k), lambda i, j: (i, 0)),
            pl.BlockSpec((1, k), lambda i, j: (0, 0)),
            pl.BlockSpec((k, tn), lambda i, j: (0, j)),
        ],
        out_specs=pl.BlockSpec((tm, tn), lambda i, j: (i, j)),
        out_shape=jax.ShapeDtypeStruct((m, n), F32),
        scratch_shapes=[pltpu.VMEM((tm, k), BF16)],
        compiler_params=pltpu.CompilerParams(
            dimension_semantics=("parallel", "arbitrary"), vmem_limit_bytes=VMEM_LIMIT),
        name=name,
    )(x2d, g.reshape(1, k), w_bf16)


def _in_proj_kernel(x_ref, g_ref, w_ref, wla_ref, o_ref, la_ref, h_ref):
    j = pl.program_id(1)

    @pl.when(j == 0)
    def _():
        for r0 in range(0, x_ref.shape[0], NORM_SUB_ROWS):
            rows = slice(r0, r0 + NORM_SUB_ROWS)
            x = x_ref[rows, :]
            ms = jnp.mean(x * x, axis=-1, keepdims=True)
            h = (x * lax.rsqrt(ms + NORM_EPS) * g_ref[...]).astype(BF16)
            h_ref[rows, :] = h
            la_ref[rows, :] = jnp.dot(h, wla_ref[...], preferred_element_type=F32)
            o_ref[rows, :] = jnp.dot(h, w_ref[...], preferred_element_type=F32)

    @pl.when(j != 0)
    def _():
        o_ref[...] = jnp.dot(h_ref[...], w_ref[...], preferred_element_type=F32)


def _in_proj(x2d, g, w_main, w_lora, tm, tn):
    m, k = x2d.shape
    n = w_main.shape[1]
    assert m % tm == 0 and n % tn == 0 and tm % NORM_SUB_ROWS == 0
    return pl.pallas_call(
        _in_proj_kernel,
        grid=(m // tm, n // tn),
        in_specs=[
            pl.BlockSpec((tm, k), lambda i, j: (i, 0)),
            pl.BlockSpec((1, k), lambda i, j: (0, 0)),
            pl.BlockSpec((k, tn), lambda i, j: (0, j)),
            pl.BlockSpec((k, 2 * LORA), lambda i, j: (0, 0)),
        ],
        out_specs=[
            pl.BlockSpec((tm, tn), lambda i, j: (i, j)),
            pl.BlockSpec((tm, 2 * LORA), lambda i, j: (i, 0)),
        ],
        out_shape=[
            jax.ShapeDtypeStruct((m, n), F32),
            jax.ShapeDtypeStruct((m, 2 * LORA), F32),
        ],
        scratch_shapes=[pltpu.VMEM((tm, k), BF16)],
        compiler_params=pltpu.CompilerParams(
            dimension_semantics=("parallel", "arbitrary"), vmem_limit_bytes=VMEM_LIMIT),
        name="in_proj",
    )(x2d, g.reshape(1, k), w_main, w_lora)


def _band_attn_kernel(q_ref, k_ref, v_ref, z_ref, bias_ref, gq_ref, gk_ref, o_ref,
                      qn_ref, kp_ref, vp_ref):
    seq = q_ref.shape[0]
    scale = A_HEAD_DIM ** -0.5
    heads = [slice(h * A_HEAD_DIM, (h + 1) * A_HEAD_DIM) for h in range(ATTN_HEADS_PER_STEP)]
    kp_ref[0:K_PAD, :] = jnp.zeros((K_PAD, kp_ref.shape[1]), BF16)
    vp_ref[0:K_PAD, :] = jnp.zeros((K_PAD, vp_ref.shape[1]), BF16)
    for sl in heads:
        q = q_ref[:, sl]
        qn_ref[:, sl] = (q * lax.rsqrt(jnp.mean(q * q, axis=-1, keepdims=True) + NORM_EPS)).astype(BF16)
        k = k_ref[:, sl]
        kn = k * lax.rsqrt(jnp.mean(k * k, axis=-1, keepdims=True) + NORM_EPS) * (gk_ref[...] * gq_ref[...] * scale)
        kp_ref[K_PAD:, sl] = kn.astype(BF16)
    vp_ref[K_PAD:, :] = v_ref[...].astype(BF16)

    def scores(c, h):
        q0 = c * CHUNK
        first = K_PAD - q0
        skip = max(first, 0) // LANES * LANES
        s = _bdot_nt(qn_ref[q0:q0 + CHUNK, heads[h]], kp_ref[q0 + skip:q0 + K_WINDOW, heads[h]]) + bias_ref[h, :, skip:]
        if first > skip:
            col = lax.broadcasted_iota(jnp.int32, s.shape, 1)
            s = jnp.where(col >= first - skip, s, NEG_INF)
        return s, skip

    chunks_per_group = ATTN_CHUNKS_IN_FLIGHT // ATTN_HEADS_PER_STEP
    for c0 in range(0, seq // CHUNK, chunks_per_group):
        group = [(c, h) for c in range(c0, c0 + chunks_per_group) for h in range(ATTN_HEADS_PER_STEP)]
        sc = [scores(c, h) for c, h in group]
        mx = [jnp.max(s, axis=-1, keepdims=True) for s, _ in sc]
        ps = [jnp.exp(s - m) for (s, _), m in zip(sc, mx)]
        ls = [jnp.sum(p, axis=-1, keepdims=True) for p in ps]
        outs = [_bdot(p, vp_ref[c * CHUNK + skip:c * CHUNK + K_WINDOW, heads[h]]) / l
                for (c, h), p, l, (_, skip) in zip(group, ps, ls, sc)]
        for (c, h), o in zip(group, outs):
            rows = slice(c * CHUNK, (c + 1) * CHUNK)
            o_ref[rows, heads[h]] = (o * _silu(z_ref[rows, heads[h]])).astype(o_ref.dtype)


def _band_bias_table(rel_bias):
    period = 1024
    q = np.arange(period)
    m_minus_r = np.where(q < K_WINDOW, q, q - period)
    dist = K_PAD - m_minus_r
    f = rel_bias[:, np.clip(dist, -REL_CLIP, REL_CLIP) + REL_CLIP].astype(F32)
    seq = jnp.tile(f, (1, CHUNK))[:, :CHUNK * (period - 1)]
    toeplitz = seq.reshape(-1, CHUNK, period - 1)[:, :, :K_WINDOW]
    in_band = np.arange(K_WINDOW)[None, None, :] >= K_SLACK
    return jnp.where(jnp.asarray(in_band), toeplitz, NEG_INF)


def _band_attn(proj3, bias_tab, gq, gk):
    bsz, seq, _ = proj3.shape
    hps = ATTN_HEADS_PER_STEP
    width = hps * A_HEAD_DIM
    hb = A_WIDTH // width
    assert seq % (CHUNK * ATTN_CHUNKS_IN_FLIGHT // hps) == 0 and A_HEADS % hps == 0

    def col(group):
        return pl.BlockSpec((None, seq, width), lambda b, h: (b, 0, group * hb + h))

    return pl.pallas_call(
        _band_attn_kernel,
        grid=(bsz, A_HEADS // hps),
        in_specs=[
            col(COL_AQ), col(COL_AK), col(COL_AV), col(COL_AZ),
            pl.BlockSpec((hps, CHUNK, K_WINDOW), lambda b, h: (h, 0, 0)),
            pl.BlockSpec((1, A_HEAD_DIM), lambda b, h: (0, 0)),
            pl.BlockSpec((1, A_HEAD_DIM), lambda b, h: (0, 0)),
        ],
        out_specs=pl.BlockSpec((None, seq, width), lambda b, h: (b, 0, h)),
        out_shape=jax.ShapeDtypeStruct((bsz, seq, A_WIDTH), BF16),
        scratch_shapes=[
            pltpu.VMEM((seq, width), BF16),
            pltpu.VMEM((seq + K_PAD, width), BF16),
            pltpu.VMEM((seq + K_PAD, width), BF16),
        ],
        compiler_params=pltpu.CompilerParams(
            dimension_semantics=("parallel", "parallel"), vmem_limit_bytes=VMEM_LIMIT),
        name="band_attn",
    )(proj3, proj3, proj3, proj3, bias_tab, gq.reshape(1, -1), gk.reshape(1, -1))


def _mem_attn_kernel(q_ref, z_ref, mk_ref, mv_ref, gq_ref, gk_ref, o_ref, kn_ref):
    scale = C_HEAD_DIM ** -0.5
    heads = [slice(h * C_HEAD_DIM, (h + 1) * C_HEAD_DIM) for h in range(C_HEADS)]

    def unit_rms(x):
        return x * lax.rsqrt(jnp.mean(x * x, axis=-1, keepdims=True) + NORM_EPS)

    @pl.when(pl.program_id(1) == 0)
    def _():
        gain = gk_ref[...] * gq_ref[...] * scale
        for sl in heads:
            kn_ref[:, sl] = (unit_rms(mk_ref[:, sl]) * gain).astype(BF16)

    qn = [unit_rms(q_ref[:, sl]) for sl in heads]
    kn = [kn_ref[:, sl] for sl in heads]
    sc = [_bdot_nt(q, k) for q, k in zip(qn, kn)]
    mx = [jnp.max(s, axis=-1, keepdims=True) for s in sc]
    ps = [jnp.exp(s - m) for s, m in zip(sc, mx)]
    ls = [jnp.sum(p, axis=-1, keepdims=True) for p in ps]
    outs = [_bdot(p, mv_ref[:, sl]) / l for p, l, sl in zip(ps, ls, heads)]
    for o, sl in zip(outs, heads):
        o_ref[:, sl] = (o * _silu(z_ref[:, sl])).astype(o_ref.dtype)


def _mem_attn(proj3, mkv3, gq, gk, ts):
    bsz, seq, _ = proj3.shape
    return pl.pallas_call(
        _mem_attn_kernel,
        grid=(bsz, seq // ts),
        in_specs=[
            pl.BlockSpec((None, ts, C_WIDTH), lambda b, s: (b, s, COL_CQ)),
            pl.BlockSpec((None, ts, C_WIDTH), lambda b, s: (b, s, COL_CZ)),
            pl.BlockSpec((None, N_MEM, C_WIDTH), lambda b, s: (b, 0, 0)),
            pl.BlockSpec((None, N_MEM, C_WIDTH), lambda b, s: (b, 0, 1)),
            pl.BlockSpec((1, C_HEAD_DIM), lambda b, s: (0, 0)),
            pl.BlockSpec((1, C_HEAD_DIM), lambda b, s: (0, 0)),
        ],
        out_specs=pl.BlockSpec((None, ts, C_WIDTH), lambda b, s: (b, s, 0)),
        out_shape=jax.ShapeDtypeStruct((bsz, seq, C_WIDTH), BF16),
        scratch_shapes=[pltpu.VMEM((N_MEM, C_WIDTH), BF16)],
        compiler_params=pltpu.CompilerParams(
            dimension_semantics=("parallel", "arbitrary"), vmem_limit_bytes=VMEM_LIMIT),
        name="mem_attn",
    )(proj3, proj3, mkv3, mkv3, gq.reshape(1, -1), gk.reshape(1, -1))


def _shift_rows(x, carry_row):
    rolled = pltpu.roll(x, 1, 0)
    row = lax.broadcasted_iota(jnp.int32, x.shape, 0)
    return jnp.where(row == 0, carry_row, rolled)


def _seg_sum(x, seg_ones2):
    hi, lo = _split2(x)
    return jnp.dot(jnp.concatenate([hi, lo], axis=1), seg_ones2, preferred_element_type=F32)


def _drain(gen):
    for _ in gen:
        pass


def _rwkv_kernel(pr_ref, pk_ref, pv_ref, pz_ref, la_ref,
                 mur_ref, muk_ref, muv_ref, mula_ref, w0_ref, a0_ref, w2a2_ref,
                 kk_ref, ka_ref, rk_ref, lnw_ref, lnb_ref,
                 o_ref,
                 st_ref, cr_ref, ck_ref, cv_ref, cla_ref):
    L = CHUNK
    nb, tb = pr_ref.shape[0], pr_ref.shape[1]
    n_chunks = tb // L
    n_pairs = B_HEADS // 2
    pair_lanes = [slice(p * LANES, (p + 1) * LANES) for p in range(n_pairs)]

    @pl.when(pl.program_id(1) == 0)
    def _():
        st_ref[...] = jnp.zeros_like(st_ref)
        cr_ref[...] = jnp.zeros_like(cr_ref)
        ck_ref[...] = jnp.zeros_like(ck_ref)
        cv_ref[...] = jnp.zeros_like(cv_ref)
        cla_ref[...] = jnp.zeros_like(cla_ref)

    lane = lax.broadcasted_iota(jnp.int32, (L, LANES), 1)
    lo_half = lane < B_HEAD_DIM
    eye_packed = ((lane & (B_HEAD_DIM - 1)) == lax.broadcasted_iota(jnp.int32, (L, LANES), 0)).astype(F32)
    lo_half2 = (lax.broadcasted_iota(jnp.int32, (L, 2 * LANES), 1) & B_HEAD_DIM) == 0
    ti = lax.broadcasted_iota(jnp.int32, (L, L), 0)
    si = lax.broadcasted_iota(jnp.int32, (L, L), 1)
    tri = (si <= ti).astype(BF16)
    tri3 = jnp.concatenate([tri, tri, tri], axis=1)
    ri = lax.broadcasted_iota(jnp.int32, (LANES, LANES), 0)
    ci = lax.broadcasted_iota(jnp.int32, (LANES, LANES), 1)
    same_head = (ri < B_HEAD_DIM) == (ci < B_HEAD_DIM)
    seg_ones = jnp.concatenate([same_head.astype(BF16)] * 2, axis=0)
    inv_n = 1.0 / B_HEAD_DIM
    mi = lax.broadcasted_iota(jnp.int32, (2 * L, 4 * L), 0)
    mj = lax.broadcasted_iota(jnp.int32, (2 * L, 4 * L), 1)
    tt, ss = mi & (L - 1), mj & (L - 1)
    tri_mask = (ss < tt) | ((mi >= L) & (ss == tt))
    w_hi, w_lo = _split2(w2a2_ref[...])
    w_hhl = jnp.concatenate([w_hi, w_hi, w_lo], axis=0)

    def stack_heads(x2):
        lo = lo_half if x2.shape[1] == LANES else lo_half2
        xb = x2.astype(BF16)
        return jnp.concatenate([jnp.where(lo, xb, 0.0), jnp.where(lo, 0.0, xb)], axis=0).astype(BF16)

    la_all = [la_ref[bi] for bi in range(nb)]
    la_all = [x + mula_ref[...] * (_shift_rows(x, cla_ref[bi:bi + 1, :]) - x)
              for bi, x in enumerate(la_all)]
    is_wd = lax.broadcasted_iota(jnp.int32, (tb, LANES), 1) < LORA
    x_hi, x_lo = _split2(jnp.concatenate([jnp.where(is_wd, jnp.tanh(x), 0.0) for x in la_all]
                                         + [jnp.where(is_wd, 0.0, x) for x in la_all], axis=0))
    lora_up = jnp.dot(jnp.concatenate([x_hi, x_lo, x_hi], axis=1), w_hhl, preferred_element_type=F32)

    prev_row = [[cr_ref[bi:bi + 1, :], ck_ref[bi:bi + 1, :], cv_ref[bi:bi + 1, :]] for bi in range(nb)]

    def prep(c, out):
        rows = slice(c * L, (c + 1) * L)
        for bi in range(nb):
            def lerp(x_ref, idx, mu_ref):
                x = x_ref[bi, rows, :]
                prev = _shift_rows(x, prev_row[bi][idx])
                prev_row[bi][idx] = x[L - 1:L, :]
                return x + mu_ref[...] * (prev - x)

            r = lerp(pr_ref, 0, mur_ref)
            k = lerp(pk_ref, 1, muk_ref)
            v = lerp(pv_ref, 2, muv_ref)
            r0 = bi * tb + c * L
            lw = -DECAY_SCALE * jax.nn.sigmoid(w0_ref[...] + lora_up[r0:r0 + L, :])
            a_sig = jax.nn.sigmoid(a0_ref[...] + lora_up[nb * tb + r0:nb * tb + r0 + L, :])
            yield
            cum = jnp.dot(tri3, jnp.concatenate(_split3(lw), axis=0), preferred_element_type=F32)
            e_pos = jnp.exp(cum)
            e_neg = jnp.exp(-cum)
            e_prev = jnp.exp(cum - lw)
            yield
            for p, sl in enumerate(pair_lanes):
                kk_raw = k[:, sl] * kk_ref[:, sl]
                kmod = k[:, sl] * (1.0 + (a_sig[:, sl] - 1.0) * ka_ref[:, sl])
                sums = _seg_sum(jnp.concatenate([kk_raw * kk_raw, r[:, sl] * kmod * rk_ref[:, sl]], axis=0),
                                seg_ones)
                kk = kk_raw * lax.rsqrt(jnp.maximum(sums[:L], 1e-24))
                bt = (kk * a_sig[:, sl]) * e_neg[:, sl]
                kt = kmod * e_neg[:, sl]
                out.append(dict(
                    bi=bi, sl=sl,
                    at=(-kk) * e_prev[:, sl], rt=r[:, sl] * e_pos[:, sl], bt=bt, kt=kt,
                    bks=jnp.concatenate([stack_heads(bt), stack_heads(kt)], axis=0),
                    vb=v[:, sl].astype(BF16), vs=stack_heads(v[:, sl]),
                    bonus=sums[L:] * v[:, sl], decay=e_pos[L - 1:L, sl]))
                if p % 2 == 1:
                    yield

    def scores(ops):
        for d in ops:
            d["sm"] = jnp.where(tri_mask, _bdot_nt(jnp.concatenate([d["at"], d["rt"]], axis=0), d["bks"]),
                                0.0).astype(BF16)
        for d in ops:
            d["av"] = jnp.dot(d["sm"][:, 2 * L:], d["vs"], preferred_element_type=F32)
            d["x"] = jnp.concatenate([d["at"], d["av"][:L]], axis=1)
            d["pw"] = d["sm"][:L, :2 * L]

    def post(c, ops, o2):
        rows = slice(c * L, (c + 1) * L)
        for u, d in enumerate(ops):
            bi, sl = d["bi"], d["sl"]
            mean = _seg_sum(o2[u], seg_ones) * inv_n
            dev = o2[u] - mean
            var = _seg_sum(dev * dev, seg_ones) * inv_n
            on = dev * lax.rsqrt(var + GN_EPS) * lnw_ref[:, sl] + lnb_ref[:, sl]
            o_ref[bi, rows, sl] = ((on + d["bonus"]) * _silu(pz_ref[bi, rows, sl])).astype(o_ref.dtype)
            if u % 2 == 1:
                yield

    st = [st_ref[u] for u in range(nb * n_pairs)]
    ops = [[] for _ in range(n_chunks)]
    _drain(prep(0, ops[0]))
    scores(ops[0])
    side = []
    for c in range(n_chunks):
        cur = ops[c]
        if c + 1 < n_chunks:
            side.append(prep(c + 1, ops[c + 1]))
        def spread():
            for gen in side:
                for _ in range(nb):
                    next(gen, None)

        for d in cur:
            d["t"] = eye_packed + d["pw"].astype(F32)
            d["pw"] = jnp.dot(d["pw"], stack_heads(d["pw"]), preferred_element_type=F32).astype(BF16)
        spread()
        n = 2
        while 2 * n < L:
            for d in cur:
                both = jnp.dot(jnp.concatenate([d["pw"], d["t"].astype(BF16)], axis=0), stack_heads(d["pw"]),
                               preferred_element_type=F32)
                d["t"] = d["t"] + both[L:]
                d["pw"] = both[:L].astype(BF16)
            n *= 2
            spread()
        for d in cur:
            d["t"] = d["t"] + jnp.dot(d["t"].astype(BF16), stack_heads(d["pw"]), preferred_element_type=F32)
        spread()
        for d in cur:
            d["x"] = jnp.dot(d["t"].astype(BF16), stack_heads(d["x"]), preferred_element_type=F32)
        for gen in side:
            _drain(gen)
        side = []
        if c + 1 < n_chunks:
            scores(ops[c + 1])
        hs = [_bdot_nt(jnp.concatenate([d["x"][:, :LANES], d["rt"]], axis=0), st[p]) for p, d in enumerate(cur)]
        us = [hs[p][:L] + d["x"][:, LANES:] for p, d in enumerate(cur)]
        o2 = [hs[p][L:] + jnp.dot(d["sm"][L:, :2 * L], stack_heads(us[p]), preferred_element_type=F32)
              + d["av"][L:] for p, d in enumerate(cur)]
        for p, d in enumerate(cur):
            upd = _bdot_tn(jnp.concatenate([us[p].astype(BF16), d["vb"]], axis=0),
                           jnp.concatenate([d["bt"], d["kt"]], axis=0))
            st[p] = (st[p] + jnp.where(same_head, upd, 0.0)) * d["decay"]
        side.append(post(c, cur, o2))
    for gen in side:
        _drain(gen)

    for u in range(nb * n_pairs):
        st_ref[u] = st[u]
    for bi in range(nb):
        cr_ref[bi:bi + 1, :], ck_ref[bi:bi + 1, :], cv_ref[bi:bi + 1, :] = prev_row[bi]
        cla_ref[bi:bi + 1, :] = la_ref[bi, tb - 1:tb, :]


def _rwkv(proj3, lora3, mu_rkv, mu_la, w0, a0, w2a2, k_k, k_a, r_k, ln_w, ln_b, tokens_per_step, batch_per_step):
    bsz, seq, _ = proj3.shape
    tb = tokens_per_step
    nb = batch_per_step
    assert nb <= 8 and bsz % nb == 0 and seq % tb == 0 and tb % CHUNK == 0

    def col(group):
        return pl.BlockSpec((nb, tb, B_WIDTH), lambda b, c: (b, c, group))

    def vec(n):
        return pl.BlockSpec((1, n), lambda b, c: (0, 0))

    row = lambda t: t.reshape(1, -1)
    return pl.pallas_call(
        _rwkv_kernel,
        grid=(bsz // nb, seq // tb),
        in_specs=[
            col(COL_BR), col(COL_BK), col(COL_BV), col(COL_BZ),
            pl.BlockSpec((nb, tb, 2 * LORA), lambda b, c: (b, c, 0)),
            vec(B_WIDTH), vec(B_WIDTH), vec(B_WIDTH), vec(2 * LORA), vec(B_WIDTH), vec(B_WIDTH),
            pl.BlockSpec((2 * LORA, B_WIDTH), lambda b, c: (0, 0)),
            vec(B_WIDTH), vec(B_WIDTH), vec(B_WIDTH), vec(B_WIDTH), vec(B_WIDTH),
        ],
        out_specs=pl.BlockSpec((nb, tb, B_WIDTH), lambda b, c: (b, c, 0)),
        out_shape=jax.ShapeDtypeStruct((bsz, seq, B_WIDTH), BF16),
        scratch_shapes=[
            pltpu.VMEM((nb * B_HEADS // 2, LANES, LANES), F32),
            pltpu.VMEM((8, B_WIDTH), F32), pltpu.VMEM((8, B_WIDTH), F32), pltpu.VMEM((8, B_WIDTH), F32),
            pltpu.VMEM((8, 2 * LORA), F32),
        ],
        compiler_params=pltpu.CompilerParams(
            dimension_semantics=("parallel", "arbitrary"), vmem_limit_bytes=VMEM_LIMIT),
        name="rwkv",
    )(proj3, proj3, proj3, proj3, lora3,
      row(mu_rkv[0]), row(mu_rkv[1]), row(mu_rkv[2]), row(mu_la), row(w0), row(a0), w2a2,
      row(k_k), row(k_a), row(r_k), row(ln_w), row(ln_b))


def _merge_out_kernel(ya_ref, yb_ref, yc_ref, ga_ref, gb_ref, gc_ref, wa_ref, wb_ref, wc_ref, wo_ref, x_ref,
                      o_ref):
    merged = (jax.nn.sigmoid(ga_ref[...]) * jnp.dot(ya_ref[...], wa_ref[...], preferred_element_type=F32)
              + jax.nn.sigmoid(gb_ref[...]) * jnp.dot(yb_ref[...], wb_ref[...], preferred_element_type=F32)
              + jax.nn.sigmoid(gc_ref[...]) * jnp.dot(yc_ref[...], wc_ref[...], preferred_element_type=F32))
    o_ref[...] = x_ref[...] + jnp.dot(merged.astype(BF16), wo_ref[...], preferred_element_type=F32)


def _merge_out(ya, yb, yc, proj2, wa, wb, wc, wo, x2d, tm):
    m = x2d.shape[0]

    def ybranch(width):
        return pl.BlockSpec((tm, width), lambda i: (i, 0))

    def gate(group):
        return pl.BlockSpec((tm, D_MODEL), lambda i: (i, group * 1024 // D_MODEL))

    def resident(shape):
        return pl.BlockSpec(shape, lambda i: (0, 0), pipeline_mode=pl.Buffered(1))

    return pl.pallas_call(
        _merge_out_kernel,
        grid=(m // tm,),
        in_specs=[
            ybranch(A_WIDTH), ybranch(B_WIDTH), ybranch(C_WIDTH),
            gate(COL_GA), gate(COL_GB), gate(COL_GC),
            resident((A_WIDTH, D_MODEL)), resident((B_WIDTH, D_MODEL)), resident((C_WIDTH, D_MODEL)),
            resident((D_MODEL, D_MODEL)),
            pl.BlockSpec((tm, D_MODEL), lambda i: (i, 0)),
        ],
        out_specs=pl.BlockSpec((tm, D_MODEL), lambda i: (i, 0)),
        out_shape=jax.ShapeDtypeStruct((m, D_MODEL), F32),
        compiler_params=pltpu.CompilerParams(
            dimension_semantics=("parallel",), vmem_limit_bytes=VMEM_LIMIT),
        name="merge_out",
    )(ya, yb, yc, proj2, proj2, proj2, wa, wb, wc, wo, x2d)


def _layer(x, mem, norm_g, w_in, a_q_g, a_k_g, a_rel_bias, w_up_a,
           b_mu_rkv, b_mu_w, b_mu_a, b_w0, b_w2, b_a0, b_a2, b_k_k, b_k_a, b_r_k,
           b_ln_w, b_ln_b, w_up_b, mem_norm_g, w_mem_kv, c_q_g, c_k_g, w_up_c, w_o):
    bsz, seq, d = x.shape
    t = bsz * seq
    x2d = x.reshape(t, d)
    w_main, w_lora = _prep_w_in(w_in, 128)
    wa, wb, wc, wo, wm = _cast_weights([w_up_a, w_up_b, w_up_c, w_o, w_mem_kv], 8)

    proj, lora = _in_proj(x2d, norm_g, w_main, w_lora, 1024, 1024)
    mkv = _norm_matmul(mem.reshape(bsz * N_MEM, d), mem_norm_g, wm, 1024, 1024, "mem_kv")

    proj3 = proj.reshape(bsz, seq, MAIN_COLS)
    ya = _band_attn(proj3, _band_bias_table(a_rel_bias), a_q_g, a_k_g)
    yc = _mem_attn(proj3, mkv.reshape(bsz, N_MEM, 2 * C_WIDTH), c_q_g, c_k_g, 512)
    yb = _rwkv(proj3, lora.reshape(bsz, seq, 2 * LORA), b_mu_rkv,
               jnp.concatenate([b_mu_w, b_mu_a]), b_w0, b_a0,
               jnp.concatenate([b_w2, b_a2], axis=0), b_k_k, b_k_a, b_r_k.reshape(-1), b_ln_w, b_ln_b, 4 * CHUNK, 2)

    out = _merge_out(ya.reshape(t, A_WIDTH), yb.reshape(t, B_WIDTH), yc.reshape(t, C_WIDTH), proj,
                     wa, wb, wc, wo, x2d, 256)
    return out.reshape(bsz, seq, d)


def kernel(x, mem, norm_g, w_in, a_q_g, a_k_g, a_rel_bias, w_up_a, b_mu_rkv, b_mu_w, b_mu_a, b_w0, b_w2, b_a0, b_a2, b_k_k, b_k_a, b_r_k, b_ln_w, b_ln_b, w_up_b, mem_norm_g, w_mem_kv, c_q_g, c_k_g, w_up_c, w_o):
    for l in range(norm_g.shape[0]):
        x = _layer(x, mem, norm_g[l], w_in[l], a_q_g[l], a_k_g[l], a_rel_bias[l], w_up_a[l],
                   b_mu_rkv[l], b_mu_w[l], b_mu_a[l], b_w0[l], b_w2[l], b_a0[l], b_a2[l],
                   b_k_k[l], b_k_a[l], b_r_k[l], b_ln_w[l], b_ln_b[l], w_up_b[l],
                   mem_norm_g[l], w_mem_kv[l], c_q_g[l], c_k_g[l], w_up_c[l], w_o[l])
    return x
```

```python
import numpy as np
import jax
import jax.numpy as jnp
from jax import lax
from jax.experimental import pallas as pl
from jax.experimental.pallas import tpu as pltpu

D_MODEL = 2048
CHUNK = 64
N_MEM = 256
NORM_EPS = 1e-6
NEG_INF = -1e30

A_HEADS = 8
A_HEAD_DIM = 128
A_WIDTH = A_HEADS * A_HEAD_DIM
A_LEFT_CHUNKS = 8
REL_CLIP = 128

B_HEADS = 16
B_HEAD_DIM = 64
B_WIDTH = B_HEADS * B_HEAD_DIM
LORA = 64
GN_EPS = 64e-5
DECAY_SCALE = float(np.exp(-0.5))

C_HEADS = 4
C_HEAD_DIM = 256
C_WIDTH = C_HEADS * C_HEAD_DIM

LANES = 128
VMEM_LIMIT = 56 * 1024 * 1024

A_BAND = (A_LEFT_CHUNKS + 1) * CHUNK
K_WINDOW = -(-A_BAND // LANES) * LANES
K_SLACK = K_WINDOW - A_BAND
K_PAD = K_WINDOW - CHUNK
ATTN_CHUNKS_IN_FLIGHT = 16
NORM_SUB_ROWS = 256

COL_AQ, COL_AK, COL_AV, COL_AZ = 0, 1, 2, 3
COL_BR, COL_BK, COL_BV, COL_BZ = 4, 5, 6, 7
COL_CQ, COL_CZ = 8, 9
COL_GA, COL_GB, COL_GC = 10, 12, 14
MAIN_COLS = 16 * 1024
LORA_LO = 4 * A_WIDTH + 4 * B_WIDTH
LORA_HI = LORA_LO + 2 * LORA

BF16 = jnp.bfloat16
F32 = jnp.float32


def _bdot(a, b):
    return jnp.dot(a.astype(BF16), b.astype(BF16), preferred_element_type=F32)


def _bdot_nt(a, b):
    return lax.dot_general(a.astype(BF16), b.astype(BF16), (((1,), (1,)), ((), ())),
                           preferred_element_type=F32)


def _bdot_tn(a, b):
    return lax.dot_general(a.astype(BF16), b.astype(BF16), (((0,), (0,)), ((), ())),
                           preferred_element_type=F32)


def _split2(x):
    hi = x.astype(BF16)
    lo = (x - hi.astype(F32)).astype(BF16)
    return hi, lo


def _split3(x):
    hi = x.astype(BF16)
    r1 = x - hi.astype(F32)
    mid = r1.astype(BF16)
    lo = (r1 - mid.astype(F32)).astype(BF16)
    return hi, mid, lo


def _silu(x):
    h = 0.5 * x
    return h + h * jnp.tanh(h)


def _prep_w_in_kernel(w_ref, main_ref, lora_ref):
    main_ref[:, :LORA_LO] = w_ref[:, :LORA_LO].astype(BF16)
    main_ref[:, LORA_LO:] = w_ref[:, LORA_HI:].astype(BF16)
    lora_ref[...] = w_ref[:, LORA_LO:LORA_HI].astype(BF16)


def _prep_w_in(w_in, rows):
    k, n = w_in.shape
    return pl.pallas_call(
        _prep_w_in_kernel,
        grid=(k // rows,),
        in_specs=[pl.BlockSpec((rows, n), lambda i: (i, 0))],
        out_specs=[
            pl.BlockSpec((rows, MAIN_COLS), lambda i: (i, 0)),
            pl.BlockSpec((rows, 2 * LORA), lambda i: (i, 0)),
        ],
        out_shape=[
            jax.ShapeDtypeStruct((k, MAIN_COLS), BF16),
            jax.ShapeDtypeStruct((k, 2 * LORA), BF16),
        ],
        compiler_params=pltpu.CompilerParams(
            dimension_semantics=("parallel",), vmem_limit_bytes=VMEM_LIMIT),
        name="prep_w_in",
    )(w_in)


def _cast_weights_kernel(*refs):
    n = len(refs) // 2
    for src, dst in zip(refs[:n], refs[n:]):
        dst[...] = src[...].astype(BF16)


def _cast_weights(ws, n_steps):
    specs = [pl.BlockSpec((w.shape[0] // n_steps, w.shape[1]), lambda i: (i, 0)) for w in ws]
    return pl.pallas_call(
        _cast_weights_kernel,
        grid=(n_steps,),
        in_specs=specs,
        out_specs=specs,
        out_shape=[jax.ShapeDtypeStruct(w.shape, BF16) for w in ws],
        compiler_params=pltpu.CompilerParams(
            dimension_semantics=("parallel",), vmem_limit_bytes=VMEM_LIMIT),
        name="cast_weights",
    )(*ws)


def _norm_matmul_kernel(x_ref, g_ref, w_ref, o_ref, h_ref):
    @pl.when(pl.program_id(1) == 0)
    def _():
        x = x_ref[...]
        ms = jnp.mean(x * x, axis=-1, keepdims=True)
        h_ref[...] = (x * lax.rsqrt(ms + NORM_EPS) * g_ref[...]).astype(BF16)

    o_ref[...] = jnp.dot(h_ref[...], w_ref[...], preferred_element_type=F32).astype(o_ref.dtype)


def _norm_matmul(x2d, g, w_bf16, tm, tn, name):
    m, k = x2d.shape
    n = w_bf16.shape[1]
    return pl.pallas_call(
        _norm_matmul_kernel,
        grid=(m // tm, n // tn),
        in_specs=[
            pl.BlockSpec((tm, k), lambda i, j: (i, 0)),
            pl.BlockSpec((1, k), lambda i, j: (0, 0)),
            pl.BlockSpec((k, tn), lambda i, j: (0, j)),
        ],
        out_specs=pl.BlockSpec((tm, tn), lambda i, j: (i, j)),
        out_shape=jax.ShapeDtypeStruct((m, n), F32),
        scratch_shapes=[pltpu.VMEM((tm, k), BF16)],
        compiler_params=pltpu.CompilerParams(
            dimension_semantics=("parallel", "arbitrary"), vmem_limit_bytes=VMEM_LIMIT),
        name=name,
    )(x2d, g.reshape(1, k), w_bf16)


def _in_proj_kernel(x_ref, g_ref, w_ref, wla_ref, o_ref, la_ref, h_ref):
    j = pl.program_id(1)

    @pl.when(j == 0)
    def _():
        for r0 in range(0, x_ref.shape[0], NORM_SUB_ROWS):
            rows = slice(r0, r0 + NORM_SUB_ROWS)
            x = x_ref[rows, :]
            ms = jnp.mean(x * x, axis=-1, keepdims=True)
            h = (x * lax.rsqrt(ms + NORM_EPS) * g_ref[...]).astype(BF16)
            h_ref[rows, :] = h
            la_ref[rows, :] = jnp.dot(h, wla_ref[...], preferred_element_type=F32)
            o_ref[rows, :] = jnp.dot(h, w_ref[...], preferred_element_type=F32)

    @pl.when(j != 0)
    def _():
        o_ref[...] = jnp.dot(h_ref[...], w_ref[...], preferred_element_type=F32)


def _in_proj(x2d, g, w_main, w_lora, tm, tn):
    m, k = x2d.shape
    n = w_main.shape[1]
    assert m % tm == 0 and n % tn == 0 and tm % NORM_SUB_ROWS == 0
    return pl.pallas_call(
        _in_proj_kernel,
        grid=(m // tm, n // tn),
        in_specs=[
            pl.BlockSpec((tm, k), lambda i, j: (i, 0)),
            pl.BlockSpec((1, k), lambda i, j: (0, 0)),
            pl.BlockSpec((k, tn), lambda i, j: (0, j)),
            pl.BlockSpec((k, 2 * LORA), lambda i, j: (0, 0)),
        ],
        out_specs=[
            pl.BlockSpec((tm, tn), lambda i, j: (i, j)),
            pl.BlockSpec((tm, 2 * LORA), lambda i, j: (i, 0)),
        ],
        out_shape=[
            jax.ShapeDtypeStruct((m, n), F32),
            jax.ShapeDtypeStruct((m, 2 * LORA), F32),
        ],
        scratch_shapes=[pltpu.VMEM((tm, k), BF16)],
        compiler_params=pltpu.CompilerParams(
            dimension_semantics=("parallel", "arbitrary"), vmem_limit_bytes=VMEM_LIMIT),
        name="in_proj",
    )(x2d, g.reshape(1, k), w_main, w_lora)


def _band_attn_kernel(q_ref, k_ref, v_ref, z_ref, bias_ref, gq_ref, gk_ref, o_ref,
                      qn_ref, kp_ref, vp_ref):
    seq = q_ref.shape[0]
    scale = A_HEAD_DIM ** -0.5
    q = q_ref[...]
    qn_ref[...] = (q * lax.rsqrt(jnp.mean(q * q, axis=-1, keepdims=True) + NORM_EPS)).astype(BF16)
    k = k_ref[...]
    kn = k * lax.rsqrt(jnp.mean(k * k, axis=-1, keepdims=True) + NORM_EPS) * (gk_ref[...] * gq_ref[...] * scale)
    kp_ref[0:K_PAD, :] = jnp.zeros((K_PAD, A_HEAD_DIM), BF16)
    vp_ref[0:K_PAD, :] = jnp.zeros((K_PAD, A_HEAD_DIM), BF16)
    kp_ref[K_PAD:, :] = kn.astype(BF16)
    vp_ref[K_PAD:, :] = v_ref[...].astype(BF16)
    def scores(c):
        q0 = c * CHUNK
        first = K_PAD - q0
        skip = max(first, 0) // LANES * LANES
        s = _bdot_nt(qn_ref[q0:q0 + CHUNK, :], kp_ref[q0 + skip:q0 + K_WINDOW, :]) + bias_ref[:, skip:]
        if first > skip:
            col = lax.broadcasted_iota(jnp.int32, s.shape, 1)
            s = jnp.where(col >= first - skip, s, NEG_INF)
        return s, skip

    for c0 in range(0, seq // CHUNK, ATTN_CHUNKS_IN_FLIGHT):
        group = range(c0, c0 + ATTN_CHUNKS_IN_FLIGHT)
        sc = [scores(c) for c in group]
        mx = [jnp.max(s, axis=-1, keepdims=True) for s, _ in sc]
        ps = [jnp.exp(s - m) for (s, _), m in zip(sc, mx)]
        ls = [jnp.sum(p, axis=-1, keepdims=True) for p in ps]
        outs = [_bdot(p, vp_ref[c * CHUNK + skip:c * CHUNK + K_WINDOW, :]) / l
                for c, p, l, (_, skip) in zip(group, ps, ls, sc)]
        for c, o in zip(group, outs):
            q0 = c * CHUNK
            o_ref[q0:q0 + CHUNK, :] = (o * _silu(z_ref[q0:q0 + CHUNK, :])).astype(o_ref.dtype)


def _band_bias_table(rel_bias):
    period = 1024
    q = np.arange(period)
    m_minus_r = np.where(q < K_WINDOW, q, q - period)
    dist = K_PAD - m_minus_r
    f = rel_bias[:, np.clip(dist, -REL_CLIP, REL_CLIP) + REL_CLIP].astype(F32)
    seq = jnp.tile(f, (1, CHUNK))[:, :CHUNK * (period - 1)]
    toeplitz = seq.reshape(-1, CHUNK, period - 1)[:, :, :K_WINDOW]
    in_band = np.arange(K_WINDOW)[None, None, :] >= K_SLACK
    return jnp.where(jnp.asarray(in_band), toeplitz, NEG_INF)


def _band_attn(proj3, bias_tab, gq, gk):
    bsz, seq, _ = proj3.shape
    hb = A_WIDTH // A_HEAD_DIM
    assert seq % (CHUNK * ATTN_CHUNKS_IN_FLIGHT) == 0

    def col(group):
        return pl.BlockSpec((None, seq, A_HEAD_DIM), lambda b, h: (b, 0, group * hb + h))

    return pl.pallas_call(
        _band_attn_kernel,
        grid=(bsz, A_HEADS),
        in_specs=[
            col(COL_AQ), col(COL_AK), col(COL_AV), col(COL_AZ),
            pl.BlockSpec((None, CHUNK, K_WINDOW), lambda b, h: (h, 0, 0)),
            pl.BlockSpec((1, A_HEAD_DIM), lambda b, h: (0, 0)),
            pl.BlockSpec((1, A_HEAD_DIM), lambda b, h: (0, 0)),
        ],
        out_specs=pl.BlockSpec((None, seq, A_HEAD_DIM), lambda b, h: (b, 0, h)),
        out_shape=jax.ShapeDtypeStruct((bsz, seq, A_WIDTH), BF16),
        scratch_shapes=[
            pltpu.VMEM((seq, A_HEAD_DIM), BF16),
            pltpu.VMEM((seq + K_PAD, A_HEAD_DIM), BF16),
            pltpu.VMEM((seq + K_PAD, A_HEAD_DIM), BF16),
        ],
        compiler_params=pltpu.CompilerParams(
            dimension_semantics=("parallel", "parallel"), vmem_limit_bytes=VMEM_LIMIT),
        name="band_attn",
    )(proj3, proj3, proj3, proj3, bias_tab, gq.reshape(1, -1), gk.reshape(1, -1))


def _mem_attn_kernel(q_ref, z_ref, mk_ref, mv_ref, gq_ref, gk_ref, o_ref, kn_ref):
    scale = C_HEAD_DIM ** -0.5
    heads = [slice(h * C_HEAD_DIM, (h + 1) * C_HEAD_DIM) for h in range(C_HEADS)]

    def unit_rms(x):
        return x * lax.rsqrt(jnp.mean(x * x, axis=-1, keepdims=True) + NORM_EPS)

    @pl.when(pl.program_id(1) == 0)
    def _():
        gain = gk_ref[...] * gq_ref[...] * scale
        for sl in heads:
            kn_ref[:, sl] = (unit_rms(mk_ref[:, sl]) * gain).astype(BF16)

    qn = [unit_rms(q_ref[:, sl]) for sl in heads]
    kn = [kn_ref[:, sl] for sl in heads]
    sc = [_bdot_nt(q, k) for q, k in zip(qn, kn)]
    mx = [jnp.max(s, axis=-1, keepdims=True) for s in sc]
    ps = [jnp.exp(s - m) for s, m in zip(sc, mx)]
    ls = [jnp.sum(p, axis=-1, keepdims=True) for p in ps]
    outs = [_bdot(p, mv_ref[:, sl]) / l for p, l, sl in zip(ps, ls, heads)]
    for o, sl in zip(outs, heads):
        o_ref[:, sl] = (o * _silu(z_ref[:, sl])).astype(o_ref.dtype)


def _mem_attn(proj3, mkv3, gq, gk, ts):
    bsz, seq, _ = proj3.shape
    return pl.pallas_call(
        _mem_attn_kernel,
        grid=(bsz, seq // ts),
        in_specs=[
            pl.BlockSpec((None, ts, C_WIDTH), lambda b, s: (b, s, COL_CQ)),
            pl.BlockSpec((None, ts, C_WIDTH), lambda b, s: (b, s, COL_CZ)),
            pl.BlockSpec((None, N_MEM, C_WIDTH), lambda b, s: (b, 0, 0)),
            pl.BlockSpec((None, N_MEM, C_WIDTH), lambda b, s: (b, 0, 1)),
            pl.BlockSpec((1, C_HEAD_DIM), lambda b, s: (0, 0)),
            pl.BlockSpec((1, C_HEAD_DIM), lambda b, s: (0, 0)),
        ],
        out_specs=pl.BlockSpec((None, ts, C_WIDTH), lambda b, s: (b, s, 0)),
        out_shape=jax.ShapeDtypeStruct((bsz, seq, C_WIDTH), BF16),
        scratch_shapes=[pltpu.VMEM((N_MEM, C_WIDTH), BF16)],
        compiler_params=pltpu.CompilerParams(
            dimension_semantics=("parallel", "arbitrary"), vmem_limit_bytes=VMEM_LIMIT),
        name="mem_attn",
    )(proj3, proj3, mkv3, mkv3, gq.reshape(1, -1), gk.reshape(1, -1))


def _shift_rows(x, carry_row):
    rolled = pltpu.roll(x, 1, 0)
    row = lax.broadcasted_iota(jnp.int32, x.shape, 0)
    return jnp.where(row == 0, carry_row, rolled)


def _seg_sum(x, seg_ones2):
    hi, lo = _split2(x)
    return jnp.dot(jnp.concatenate([hi, lo], axis=1), seg_ones2, preferred_element_type=F32)


def _drain(gen):
    for _ in gen:
        pass


def _rwkv_kernel(pr_ref, pk_ref, pv_ref, pz_ref, la_ref,
                 mur_ref, muk_ref, muv_ref, mula_ref, w0_ref, a0_ref, w2a2_ref,
                 kk_ref, ka_ref, rk_ref, lnw_ref, lnb_ref,
                 o_ref,
                 st_ref, cr_ref, ck_ref, cv_ref, cla_ref):
    L = CHUNK
    nb, tb = pr_ref.shape[0], pr_ref.shape[1]
    n_chunks = tb // L
    n_pairs = B_HEADS // 2
    pair_lanes = [slice(p * LANES, (p + 1) * LANES) for p in range(n_pairs)]

    @pl.when(pl.program_id(1) == 0)
    def _():
        st_ref[...] = jnp.zeros_like(st_ref)
        cr_ref[...] = jnp.zeros_like(cr_ref)
        ck_ref[...] = jnp.zeros_like(ck_ref)
        cv_ref[...] = jnp.zeros_like(cv_ref)
        cla_ref[...] = jnp.zeros_like(cla_ref)

    lane = lax.broadcasted_iota(jnp.int32, (L, LANES), 1)
    lo_half = lane < B_HEAD_DIM
    eye_packed = ((lane & (B_HEAD_DIM - 1)) == lax.broadcasted_iota(jnp.int32, (L, LANES), 0)).astype(F32)
    lo_half2 = (lax.broadcasted_iota(jnp.int32, (L, 2 * LANES), 1) & B_HEAD_DIM) == 0
    ti = lax.broadcasted_iota(jnp.int32, (L, L), 0)
    si = lax.broadcasted_iota(jnp.int32, (L, L), 1)
    tri = (si <= ti).astype(BF16)
    tri3 = jnp.concatenate([tri, tri, tri], axis=1)
    ri = lax.broadcasted_iota(jnp.int32, (LANES, LANES), 0)
    ci = lax.broadcasted_iota(jnp.int32, (LANES, LANES), 1)
    same_head = (ri < B_HEAD_DIM) == (ci < B_HEAD_DIM)
    seg_ones = jnp.concatenate([same_head.astype(BF16)] * 2, axis=0)
    inv_n = 1.0 / B_HEAD_DIM
    mi = lax.broadcasted_iota(jnp.int32, (2 * L, 4 * L), 0)
    mj = lax.broadcasted_iota(jnp.int32, (2 * L, 4 * L), 1)
    tt, ss = mi & (L - 1), mj & (L - 1)
    tri_mask = (ss < tt) | ((mi >= L) & (ss == tt))
    w_hi, w_lo = _split2(w2a2_ref[...])
    w_hhl = jnp.concatenate([w_hi, w_hi, w_lo], axis=0)

    def stack_heads(x2):
        lo = lo_half if x2.shape[1] == LANES else lo_half2
        xb = x2.astype(BF16)
        return jnp.concatenate([jnp.where(lo, xb, 0.0), jnp.where(lo, 0.0, xb)], axis=0).astype(BF16)

    la_all = [la_ref[bi] for bi in range(nb)]
    la_all = [x + mula_ref[...] * (_shift_rows(x, cla_ref[bi:bi + 1, :]) - x)
              for bi, x in enumerate(la_all)]
    is_wd = lax.broadcasted_iota(jnp.int32, (tb, LANES), 1) < LORA
    x_hi, x_lo = _split2(jnp.concatenate([jnp.where(is_wd, jnp.tanh(x), 0.0) for x in la_all]
                                         + [jnp.where(is_wd, 0.0, x) for x in la_all], axis=0))
    lora_up = jnp.dot(jnp.concatenate([x_hi, x_lo, x_hi], axis=1), w_hhl, preferred_element_type=F32)

    prev_row = [[cr_ref[bi:bi + 1, :], ck_ref[bi:bi + 1, :], cv_ref[bi:bi + 1, :]] for bi in range(nb)]

    def prep(c, out):
        rows = slice(c * L, (c + 1) * L)
        for bi in range(nb):
            def lerp(x_ref, idx, mu_ref):
                x = x_ref[bi, rows, :]
                prev = _shift_rows(x, prev_row[bi][idx])
                prev_row[bi][idx] = x[L - 1:L, :]
                return x + mu_ref[...] * (prev - x)

            r = lerp(pr_ref, 0, mur_ref)
            k = lerp(pk_ref, 1, muk_ref)
            v = lerp(pv_ref, 2, muv_ref)
            r0 = bi * tb + c * L
            lw = -DECAY_SCALE * jax.nn.sigmoid(w0_ref[...] + lora_up[r0:r0 + L, :])
            a_sig = jax.nn.sigmoid(a0_ref[...] + lora_up[nb * tb + r0:nb * tb + r0 + L, :])
            yield
            cum = jnp.dot(tri3, jnp.concatenate(_split3(lw), axis=0), preferred_element_type=F32)
            e_pos = jnp.exp(cum)
            e_neg = jnp.exp(-cum)
            e_prev = jnp.exp(cum - lw)
            yield
            for p, sl in enumerate(pair_lanes):
                kk_raw = k[:, sl] * kk_ref[:, sl]
                kmod = k[:, sl] * (1.0 + (a_sig[:, sl] - 1.0) * ka_ref[:, sl])
                sums = _seg_sum(jnp.concatenate([kk_raw * kk_raw, r[:, sl] * kmod * rk_ref[:, sl]], axis=0),
                                seg_ones)
                kk = kk_raw * lax.rsqrt(jnp.maximum(sums[:L], 1e-24))
                bt = (kk * a_sig[:, sl]) * e_neg[:, sl]
                kt = kmod * e_neg[:, sl]
                out.append(dict(
                    bi=bi, sl=sl,
                    at=(-kk) * e_prev[:, sl], rt=r[:, sl] * e_pos[:, sl], bt=bt, kt=kt,
                    bks=jnp.concatenate([stack_heads(bt), stack_heads(kt)], axis=0),
                    vb=v[:, sl].astype(BF16), vs=stack_heads(v[:, sl]),
                    bonus=sums[L:] * v[:, sl], decay=e_pos[L - 1:L, sl]))
                if p % 2 == 1:
                    yield

    def scores(ops):
        for d in ops:
            d["sm"] = jnp.where(tri_mask, _bdot_nt(jnp.concatenate([d["at"], d["rt"]], axis=0), d["bks"]),
                                0.0).astype(BF16)
        for d in ops:
            d["av"] = jnp.dot(d["sm"][:, 2 * L:], d["vs"], preferred_element_type=F32)
            d["x"] = jnp.concatenate([d["at"], d["av"][:L]], axis=1)
            d["pw"] = d["sm"][:L, :2 * L]

    def post(c, ops, o2):
        rows = slice(c * L, (c + 1) * L)
        for u, d in enumerate(ops):
            bi, sl = d["bi"], d["sl"]
            mean = _seg_sum(o2[u], seg_ones) * inv_n
            dev = o2[u] - mean
            var = _seg_sum(dev * dev, seg_ones) * inv_n
            on = dev * lax.rsqrt(var + GN_EPS) * lnw_ref[:, sl] + lnb_ref[:, sl]
            o_ref[bi, rows, sl] = ((on + d["bonus"]) * _silu(pz_ref[bi, rows, sl])).astype(o_ref.dtype)
            if u % 2 == 1:
                yield

    st = [st_ref[u] for u in range(nb * n_pairs)]
    ops = [[] for _ in range(n_chunks)]
    _drain(prep(0, ops[0]))
    scores(ops[0])
    side = []
    for c in range(n_chunks):
        cur = ops[c]
        if c + 1 < n_chunks:
            side.append(prep(c + 1, ops[c + 1]))
        def spread():
            for gen in side:
                for _ in range(nb):
                    next(gen, None)

        for d in cur:
            d["t"] = eye_packed + d["pw"].astype(F32)
            d["pw"] = jnp.dot(d["pw"], stack_heads(d["pw"]), preferred_element_type=F32).astype(BF16)
        spread()
        n = 2
        while 2 * n < L:
            for d in cur:
                both = jnp.dot(jnp.concatenate([d["pw"], d["t"].astype(BF16)], axis=0), stack_heads(d["pw"]),
                               preferred_element_type=F32)
                d["t"] = d["t"] + both[L:]
                d["pw"] = both[:L].astype(BF16)
            n *= 2
            spread()
        for d in cur:
            d["t"] = d["t"] + jnp.dot(d["t"].astype(BF16), stack_heads(d["pw"]), preferred_element_type=F32)
        spread()
        for d in cur:
            d["x"] = jnp.dot(d["t"].astype(BF16), stack_heads(d["x"]), preferred_element_type=F32)
        for gen in side:
            _drain(gen)
        side = []
        if c + 1 < n_chunks:
            scores(ops[c + 1])
        hs = [_bdot_nt(jnp.concatenate([d["x"][:, :LANES], d["rt"]], axis=0), st[p]) for p, d in enumerate(cur)]
        us = [hs[p][:L] + d["x"][:, LANES:] for p, d in enumerate(cur)]
        o2 = [hs[p][L:] + jnp.dot(d["sm"][L:, :2 * L], stack_heads(us[p]), preferred_element_type=F32)
              + d["av"][L:] for p, d in enumerate(cur)]
        for p, d in enumerate(cur):
            upd = _bdot_tn(jnp.concatenate([us[p].astype(BF16), d["vb"]], axis=0),
                           jnp.concatenate([d["bt"], d["kt"]], axis=0))
            st[p] = (st[p] + jnp.where(same_head, upd, 0.0)) * d["decay"]
        side.append(post(c, cur, o2))
    for gen in side:
        _drain(gen)

    for u in range(nb * n_pairs):
        st_ref[u] = st[u]
    for bi in range(nb):
        cr_ref[bi:bi + 1, :], ck_ref[bi:bi + 1, :], cv_ref[bi:bi + 1, :] = prev_row[bi]
        cla_ref[bi:bi + 1, :] = la_ref[bi, tb - 1:tb, :]


def _rwkv(proj3, lora3, mu_rkv, mu_la, w0, a0, w2a2, k_k, k_a, r_k, ln_w, ln_b, tokens_per_step, batch_per_step):
    bsz, seq, _ = proj3.shape
    tb = tokens_per_step
    nb = batch_per_step
    assert nb <= 8 and bsz % nb == 0 and seq % tb == 0 and tb % CHUNK == 0

    def col(group):
        return pl.BlockSpec((nb, tb, B_WIDTH), lambda b, c: (b, c, group))

    def vec(n):
        return pl.BlockSpec((1, n), lambda b, c: (0, 0))

    row = lambda t: t.reshape(1, -1)
    return pl.pallas_call(
        _rwkv_kernel,
        grid=(bsz // nb, seq // tb),
        in_specs=[
            col(COL_BR), col(COL_BK), col(COL_BV), col(COL_BZ),
            pl.BlockSpec((nb, tb, 2 * LORA), lambda b, c: (b, c, 0)),
            vec(B_WIDTH), vec(B_WIDTH), vec(B_WIDTH), vec(2 * LORA), vec(B_WIDTH), vec(B_WIDTH),
            pl.BlockSpec((2 * LORA, B_WIDTH), lambda b, c: (0, 0)),
            vec(B_WIDTH), vec(B_WIDTH), vec(B_WIDTH), vec(B_WIDTH), vec(B_WIDTH),
        ],
        out_specs=pl.BlockSpec((nb, tb, B_WIDTH), lambda b, c: (b, c, 0)),
        out_shape=jax.ShapeDtypeStruct((bsz, seq, B_WIDTH), BF16),
        scratch_shapes=[
            pltpu.VMEM((nb * B_HEADS // 2, LANES, LANES), F32),
            pltpu.VMEM((8, B_WIDTH), F32), pltpu.VMEM((8, B_WIDTH), F32), pltpu.VMEM((8, B_WIDTH), F32),
            pltpu.VMEM((8, 2 * LORA), F32),
        ],
        compiler_params=pltpu.CompilerParams(
            dimension_semantics=("parallel", "arbitrary"), vmem_limit_bytes=VMEM_LIMIT),
        name="rwkv",
    )(proj3, proj3, proj3, proj3, lora3,
      row(mu_rkv[0]), row(mu_rkv[1]), row(mu_rkv[2]), row(mu_la), row(w0), row(a0), w2a2,
      row(k_k), row(k_a), row(r_k), row(ln_w), row(ln_b))


def _merge_out_kernel(ya_ref, yb_ref, yc_ref, ga_ref, gb_ref, gc_ref, wa_ref, wb_ref, wc_ref, wo_ref, x_ref,
                      o_ref):
    merged = (jax.nn.sigmoid(ga_ref[...]) * jnp.dot(ya_ref[...], wa_ref[...], preferred_element_type=F32)
              + jax.nn.sigmoid(gb_ref[...]) * jnp.dot(yb_ref[...], wb_ref[...], preferred_element_type=F32)
              + jax.nn.sigmoid(gc_ref[...]) * jnp.dot(yc_ref[...], wc_ref[...], preferred_element_type=F32))
    o_ref[...] = x_ref[...] + jnp.dot(merged.astype(BF16), wo_ref[...], preferred_element_type=F32)


def _merge_out(ya, yb, yc, proj2, wa, wb, wc, wo, x2d, tm):
    m = x2d.shape[0]

    def ybranch(width):
        return pl.BlockSpec((tm, width), lambda i: (i, 0))

    def gate(group):
        return pl.BlockSpec((tm, D_MODEL), lambda i: (i, group * 1024 // D_MODEL))

    def resident(shape):
        return pl.BlockSpec(shape, lambda i: (0, 0), pipeline_mode=pl.Buffered(1))

    return pl.pallas_call(
        _merge_out_kernel,
        grid=(m // tm,),
        in_specs=[
            ybranch(A_WIDTH), ybranch(B_WIDTH), ybranch(C_WIDTH),
            gate(COL_GA), gate(COL_GB), gate(COL_GC),
            resident((A_WIDTH, D_MODEL)), resident((B_WIDTH, D_MODEL)), resident((C_WIDTH, D_MODEL)),
            resident((D_MODEL, D_MODEL)),
            pl.BlockSpec((tm, D_MODEL), lambda i: (i, 0)),
        ],
        out_specs=pl.BlockSpec((tm, D_MODEL), lambda i: (i, 0)),
        out_shape=jax.ShapeDtypeStruct((m, D_MODEL), F32),
        compiler_params=pltpu.CompilerParams(
            dimension_semantics=("parallel",), vmem_limit_bytes=VMEM_LIMIT),
        name="merge_out",
    )(ya, yb, yc, proj2, proj2, proj2, wa, wb, wc, wo, x2d)


def _layer(x, mem, norm_g, w_in, a_q_g, a_k_g, a_rel_bias, w_up_a,
           b_mu_rkv, b_mu_w, b_mu_a, b_w0, b_w2, b_a0, b_a2, b_k_k, b_k_a, b_r_k,
           b_ln_w, b_ln_b, w_up_b, mem_norm_g, w_mem_kv, c_q_g, c_k_g, w_up_c, w_o):
    bsz, seq, d = x.shape
    t = bsz * seq
    x2d = x.reshape(t, d)
    w_main, w_lora = _prep_w_in(w_in, 128)
    wa, wb, wc, wo, wm = _cast_weights([w_up_a, w_up_b, w_up_c, w_o, w_mem_kv], 8)

    proj, lora = _in_proj(x2d, norm_g, w_main, w_lora, 1024, 1024)
    mkv = _norm_matmul(mem.reshape(bsz * N_MEM, d), mem_norm_g, wm, 1024, 1024, "mem_kv")

    proj3 = proj.reshape(bsz, seq, MAIN_COLS)
    ya = _band_attn(proj3, _band_bias_table(a_rel_bias), a_q_g, a_k_g)
    yc = _mem_attn(proj3, mkv.reshape(bsz, N_MEM, 2 * C_WIDTH), c_q_g, c_k_g, 1024)
    yb = _rwkv(proj3, lora.reshape(bsz, seq, 2 * LORA), b_mu_rkv,
               jnp.concatenate([b_mu_w, b_mu_a]), b_w0, b_a0,
               jnp.concatenate([b_w2, b_a2], axis=0), b_k_k, b_k_a, b_r_k.reshape(-1), b_ln_w, b_ln_b, 4 * CHUNK, 2)

    out = _merge_out(ya.reshape(t, A_WIDTH), yb.reshape(t, B_WIDTH), yc.reshape(t, C_WIDTH), proj,
                     wa, wb, wc, wo, x2d, 256)
    return out.reshape(bsz, seq, d)


def kernel(x, mem, norm_g, w_in, a_q_g, a_k_g, a_rel_bias, w_up_a, b_mu_rkv, b_mu_w, b_mu_a, b_w0, b_w2, b_a0, b_a2, b_k_k, b_k_a, b_r_k, b_ln_w, b_ln_b, w_up_b, mem_norm_g, w_mem_kv, c_q_g, c_k_g, w_up_c, w_o):
    for l in range(norm_g.shape[0]):
        x = _layer(x, mem, norm_g[l], w_in[l], a_q_g[l], a_k_g[l], a_rel_bias[l], w_up_a[l],
                   b_mu_rkv[l], b_mu_w[l], b_mu_a[l], b_w0[l], b_w2[l], b_a0[l], b_a2[l],
                   b_k_k[l], b_k_a[l], b_r_k[l], b_ln_w[l], b_ln_b[l], w_up_b[l],
                   mem_norm_g[l], w_mem_kv[l], c_q_g[l], c_k_g[l], w_up_c[l], w_o[l])
    return x
```
